```python
import math, functools
import jax, jax.numpy as jnp
from jax import lax
import numpy as np

D_MODEL = 1024
BATCH = 8
SEQ = 4096
DEPTH = 2
DEC_BATCH = 32
DEC_SEQ = 8
PAST_LEN = 16384
PAGE_SIZE = 128

D_A = D_MODEL
HEAD_A = 64
H_A = D_A // HEAD_A
LORA_W = 64
LORA_A = 64
LORA_G = 160
SHIFT_W = 3 * D_A + LORA_W + LORA_A + LORA_G
GN_EPS = 64e-5
D_B = D_MODEL
CHUNK = 128
G_B = 8
C_B = D_B // G_B
HEAD_C = 64
H_C = D_MODEL // HEAD_C
D_C = H_C * HEAD_C
WINDOWS = (128, 512, 2048)
DILATIONS = (1, 4, 16)
N_DIL = len(WINDOWS)
ATTN_BLOCK = 128
N_BRANCH = 3
D_FF = 4 * D_MODEL
N_IN = SHIFT_W + 2 * D_B + 3 * N_DIL * D_C + N_BRANCH * D_MODEL
NORM_EPS = 1e-6
LN_EPS = 1e-5

kernel_name = "hybrid_rwkv7_gmlp_dilated_attn_step"


def _rmsnorm(x, g):
    x32 = x.astype(jnp.float32)
    y = x32 * lax.rsqrt(jnp.mean(x32 * x32, axis=-1, keepdims=True) + NORM_EPS)
    return (y * g.astype(jnp.float32)).astype(x.dtype)


def _layernorm(x, w, b, eps):
    x32 = x.astype(jnp.float32)
    mu = jnp.mean(x32, axis=-1, keepdims=True)
    xc = x32 - mu
    var = jnp.mean(xc * xc, axis=-1, keepdims=True)
    y = xc * lax.rsqrt(var + eps) * w.astype(jnp.float32) + b.astype(jnp.float32)
    return y.astype(x.dtype)


def _alibi_slopes(n):
    return jnp.exp2(-8.0 * jnp.arange(1, n + 1, dtype=jnp.float32) / n)


def _wkv7_scan(s0, r, w, k, v, kk, b):
    def step(s, inp):
        r_t, w_t, k_t, v_t, kk_t, b_t = inp
        sa = jnp.einsum('bhvk,bhk->bhv', s, kk_t)
        s = s * w_t[:, :, None, :] - sa[..., None] * b_t[:, :, None, :] + v_t[..., None] * k_t[:, :, None, :]
        y = jnp.einsum('bhvk,bhk->bhv', s, r_t)
        return s, y
    xs = tuple(jnp.moveaxis(a, 1, 0) for a in (r, w, k, v, kk, b))
    s, ys = lax.scan(step, s0, xs)
    return jnp.moveaxis(ys, 0, 1), s


def _rwkv7_branch(zr, shift_prev, wkv_prev, lp):
    B, T, _ = zr.shape
    f32 = jnp.float32
    prev = jnp.concatenate([shift_prev[:, None, :].astype(zr.dtype), zr[:, :-1]], axis=1)
    h = zr + (prev - zr) * lp['mu_shift']
    e1, e2, e3 = D_A, 2 * D_A, 3 * D_A
    e4, e5 = e3 + LORA_W, e3 + LORA_W + LORA_A
    r, k, v = h[..., :e1], h[..., e1:e2], h[..., e2:e3]
    hw, ha, hg = h[..., e3:e4], h[..., e4:e5], h[..., e5:]
    w_log = -jax.nn.softplus(-(lp['decay_w0'] + jnp.tanh(hw) @ lp['decay_w2']).astype(f32)) - 0.5
    decay = jnp.exp(-jnp.exp(w_log))
    a = jax.nn.sigmoid((lp['iclr_a0'] + ha @ lp['iclr_a2']).astype(f32))
    g = jax.nn.sigmoid(hg) @ lp['gate_g2']
    heads = lambda t: t.astype(f32).reshape(B, T, H_A, HEAD_A)
    kk = heads(k * lp['k_k'])
    kk = kk / jnp.maximum(jnp.sqrt(jnp.sum(kk * kk, axis=-1, keepdims=True)), 1e-12)
    k2 = k.astype(f32) * (1.0 + (a - 1.0) * lp['k_a'].astype(f32))
    rh, kh, vh, ah, wh = heads(r), heads(k2), heads(v), heads(a), heads(decay)
    y, wkv = _wkv7_scan(wkv_prev.astype(f32), rh, wh, kh, vh, kk, kk * ah)
    y = _layernorm(y, lp['lnx_w'].reshape(H_A, HEAD_A), lp['lnx_b'].reshape(H_A, HEAD_A), GN_EPS)
    bonus = jnp.sum(rh * kh * lp['r_k'].astype(f32), axis=-1, keepdims=True) * vh
    out = ((y + bonus).reshape(B, T, D_A) * g.astype(f32)).astype(zr.dtype)
    return out, zr[:, -1], wkv


def _gmlp_prompt(zg, lp):
    B, T, _ = zg.shape
    zg = jax.nn.gelu(zg, approximate=False)
    u, vv = zg[..., :D_B], zg[..., D_B:]
    vn = _layernorm(vv, lp['lnv_w'], lp['lnv_b'], LN_EPS)
    ws = jnp.tril(lp['w_s'])
    vc = vn.reshape(B, T // CHUNK, CHUNK, G_B, C_B)
    mixed = jnp.einsum('gij,bnjgc->bnigc', ws, vc) + lp['b_s'].T[:, :, None]
    return u * mixed.reshape(B, T, D_B)


def _gmlp_sample(zg, lp):
    B, S, _ = zg.shape
    zg = jax.nn.gelu(zg, approximate=False)
    u, vv = zg[..., :D_B], zg[..., D_B:]
    vn = _layernorm(vv, lp['lnv_w'], lp['lnv_b'], LN_EPS)
    ws = jnp.tril(lp['w_s'])[:, :S, :S]
    mixed = jnp.einsum('gij,bjgc->bigc', ws, vn.reshape(B, S, G_B, C_B)) + lp['b_s'][:, :S].T[:, :, None]
    return u * mixed.reshape(B, S, D_B), vn


def _dilated_prompt_one(q, k, v, slopes, dil, n_back):
    T, H, Dh = q.shape
    f32 = jnp.float32
    L = T // dil
    qb = math.gcd(L, ATTN_BLOCK)
    nb = L // qb
    span = qb + n_back
    qr = q.astype(f32).reshape(nb, qb, dil, H, Dh)
    pad = ((n_back, 0), (0, 0), (0, 0), (0, 0))
    kp = jnp.pad(k.astype(f32).reshape(L, dil, H, Dh), pad)
    vp = jnp.pad(v.astype(f32).reshape(L, dil, H, Dh), pad)
    idx = np.arange(nb)[:, None] * qb + np.arange(span)[None, :]
    kb, vb = kp[idx], vp[idx]
    rel = np.arange(qb)[:, None] + n_back - np.arange(span)[None, :]
    start = np.arange(nb)[:, None] * qb - n_back + np.arange(span)[None, :]
    valid = ((rel >= 0) & (rel <= n_back))[None] & (start >= 0)[:, None, :]
    bias = -slopes[:, None, None] * jnp.asarray(dil * rel, f32)[None]
    s = jnp.einsum('nqrhd,nkrhd->nrhqk', qr, kb) * (Dh ** -0.5) + bias
    s = jnp.where(valid[:, None, None], s, -jnp.inf)
    lse = jax.nn.logsumexp(s, axis=-1)
    p = jnp.exp(s - lse[..., None])
    o = jnp.einsum('nrhqk,nkrhd->nqrhd', p, vb).reshape(T, H, Dh)
    return o, lse.transpose(0, 3, 1, 2).reshape(T, H)


def _dilated_sample(q, kf, vf, n_past, slopes, dil, n_back):
    B, S, H, Dh = q.shape
    f32 = jnp.float32
    steps = np.arange(n_back + 1)
    c = np.arange(S)[:, None] + n_past - dil * steps[None, :]
    valid = c >= 0
    cc = np.maximum(c, 0)
    kg, vg = kf.astype(f32)[:, cc], vf.astype(f32)[:, cc]
    bias = -slopes[:, None] * jnp.asarray(dil * steps, f32)[None, :]
    s = jnp.einsum('bshd,bsnhd->bhsn', q.astype(f32), kg) * (Dh ** -0.5) + bias[None, :, None, :]
    s = jnp.where(valid[None, None], s, -jnp.inf)
    lse = jax.nn.logsumexp(s, axis=-1)
    p = jnp.exp(s - lse[..., None])
    o = jnp.einsum('bhsn,bsnhd->bshd', p, vg)
    return o, lse.transpose(0, 2, 1)


def _combine_dilations(outs, lses):
    wts = jax.nn.softmax(jnp.stack(lses), axis=0)
    return jnp.sum(wts[..., None] * jnp.stack(outs), axis=0)


def _attn_prompt(za, slopes):
    B, T, _ = za.shape
    zq = za.reshape(B, T, N_DIL, 3, H_C, HEAD_C)
    outs, lses, bufs = [], [], []
    for gi in range(N_DIL):
        q, k, v = zq[:, :, gi, 0], zq[:, :, gi, 1], zq[:, :, gi, 2]
        fn = functools.partial(_dilated_prompt_one, slopes=slopes, dil=DILATIONS[gi],
                               n_back=WINDOWS[gi] // DILATIONS[gi])
        o, lse = lax.map(lambda qkv: fn(*qkv), (q, k, v))
        outs.append(o)
        lses.append(lse)
        n_keep = min(WINDOWS[gi], T)
        bufs.append(jnp.stack([k[:, T - n_keep:], v[:, T - n_keep:]], axis=1))
    o = _combine_dilations(outs, lses)
    return o.reshape(B, T, D_C).astype(za.dtype), bufs


def _attn_sample(za, kv_prev, slopes):
    B, S, _ = za.shape
    zq = za.reshape(B, S, N_DIL, 3, H_C, HEAD_C)
    outs, lses, bufs = [], [], []
    for gi in range(N_DIL):
        q, k, v = zq[:, :, gi, 0], zq[:, :, gi, 1], zq[:, :, gi, 2]
        buf = kv_prev[gi]
        n_past = buf.shape[2]
        kf = jnp.concatenate([buf[:, 0].astype(k.dtype), k], axis=1)
        vf = jnp.concatenate([buf[:, 1].astype(v.dtype), v], axis=1)
        o, lse = _dilated_sample(q, kf, vf, n_past, slopes, DILATIONS[gi], WINDOWS[gi] // DILATIONS[gi])
        outs.append(o)
        lses.append(lse)
        bufs.append(jnp.stack([kf[:, -n_past:], vf[:, -n_past:]], axis=1))
    o = _combine_dilations(outs, lses)
    return o.reshape(B, S, D_C).astype(za.dtype), bufs


def _merge(o_a, o_b, o_c, zgate, lp):
    B, T, _ = o_a.shape
    ob = jnp.stack([o_a, o_b, o_c], axis=2)
    gates = jax.nn.sigmoid(zgate + lp['b_gate']).reshape(B, T, N_BRANCH, D_MODEL)
    merged = jnp.einsum('btnc,ncd->btnd', ob, lp['w_branch'])
    return jnp.einsum('btnd,de->bte', merged * gates, lp['w_out'])


def _sqrelu_mlp(x, w_up, w_down):
    h = jax.nn.relu(x @ w_up)
    return (h * h) @ w_down


def _block(x, lp, shift_prev, wkv_prev, kv_prev):
    xn = _rmsnorm(x, lp['ln_mix'])
    z = xn @ lp['w_in']
    e1 = SHIFT_W
    e2 = e1 + 2 * D_B
    e3 = e2 + 3 * N_DIL * D_C
    zr, zg, za, zgate = z[..., :e1], z[..., e1:e2], z[..., e2:e3], z[..., e3:]
    o_a, shift_last, wkv = _rwkv7_branch(zr, shift_prev, wkv_prev, lp)
    slopes = _alibi_slopes(H_C)
    if kv_prev is None:
        o_b = _gmlp_prompt(zg, lp)
        v_rows = None
        o_c, kv_new = _attn_prompt(za, slopes)
    else:
        o_b, v_rows = _gmlp_sample(zg, lp)
        o_c, kv_new = _attn_sample(za, kv_prev, slopes)
    x = x + _merge(o_a, o_b, o_c, zgate, lp)
    x = x + _sqrelu_mlp(_rmsnorm(x, lp['ln_mlp']), lp['w_up'], lp['w_down'])
    return x, shift_last, wkv, kv_new, v_rows


def setup_inputs(seed: int = 0) -> dict:
    key = jax.random.key(seed)
    ks = iter(jax.random.split(key, 40))
    f32 = jnp.float32

    def nrm(shape, scale=1.0):
        return jax.random.normal(next(ks), shape, f32) * scale

    def gain(shape):
        return 1.0 + nrm(shape, 0.01)

    def buf(w):
        return (DEPTH, DEC_BATCH, 2, min(w, PAST_LEN), H_C, HEAD_C)

    return {
        'x_prompt': nrm((BATCH, SEQ, D_MODEL)),
        'x_sample': nrm((DEC_BATCH, DEC_SEQ, D_MODEL)),
        'state_wkv': nrm((DEPTH, DEC_BATCH, H_A, HEAD_A, HEAD_A), 0.5),
        'state_shift': nrm((DEPTH, DEC_BATCH, SHIFT_W)),
        'cache_kv_w128': nrm(buf(WINDOWS[0])),
        'cache_kv_w512': nrm(buf(WINDOWS[1])),
        'cache_kv_w2048': nrm(buf(WINDOWS[2])),
        'ln_mix': gain((DEPTH, D_MODEL)),
        'w_in': nrm((DEPTH, D_MODEL, N_IN), D_MODEL ** -0.5),
        'mu_shift': jax.random.uniform(next(ks), (DEPTH, SHIFT_W), f32),
        'decay_w0': nrm((DEPTH, D_A), 0.5) - 1.0,
        'decay_w2': nrm((DEPTH, LORA_W, D_A), LORA_W ** -0.5),
        'iclr_a0': nrm((DEPTH, D_A), 0.1),
        'iclr_a2': nrm((DEPTH, LORA_A, D_A), LORA_A ** -0.5),
        'gate_g2': nrm((DEPTH, LORA_G, D_A), LORA_G ** -0.5),
        'k_k': 0.85 + nrm((DEPTH, D_A), 0.05),
        'k_a': 1.0 + nrm((DEPTH, D_A), 0.05),
        'r_k': nrm((DEPTH, H_A, HEAD_A), 0.1),
        'lnx_w': gain((DEPTH, D_A)),
        'lnx_b': nrm((DEPTH, D_A), 0.01),
        'lnv_w': gain((DEPTH, D_B)),
        'lnv_b': nrm((DEPTH, D_B), 0.01),
        'w_s': nrm((DEPTH, G_B, CHUNK, CHUNK), CHUNK ** -0.5),
        'b_s': gain((DEPTH, G_B, CHUNK)),
        'b_gate': nrm((DEPTH, N_BRANCH * D_MODEL), 0.01),
        'w_branch': nrm((DEPTH, N_BRANCH, D_MODEL, D_MODEL), D_MODEL ** -0.5),
        'w_out': nrm((DEPTH, D_MODEL, D_MODEL), D_MODEL ** -0.5),
        'ln_mlp': gain((DEPTH, D_MODEL)),
        'w_up': nrm((DEPTH, D_MODEL, D_FF), D_MODEL ** -0.5),
        'w_down': nrm((DEPTH, D_FF, D_MODEL), D_FF ** -0.5),
        'ln_final': gain((D_MODEL,)),
    }


def reference(x_prompt, x_sample, state_wkv, state_shift, cache_kv_w128, cache_kv_w512, cache_kv_w2048,
              ln_mix, w_in, mu_shift, decay_w0, decay_w2, iclr_a0, iclr_a2, gate_g2, k_k, k_a, r_k,
              lnx_w, lnx_b, lnv_w, lnv_b, w_s, b_s, b_gate, w_branch, w_out, ln_mlp, w_up, w_down,
              ln_final):
    def layer_params(l):
        return dict(ln_mix=ln_mix[l], w_in=w_in[l], mu_shift=mu_shift[l], decay_w0=decay_w0[l],
                    decay_w2=decay_w2[l], iclr_a0=iclr_a0[l], iclr_a2=iclr_a2[l], gate_g2=gate_g2[l],
                    k_k=k_k[l], k_a=k_a[l], r_k=r_k[l], lnx_w=lnx_w[l], lnx_b=lnx_b[l],
                    lnv_w=lnv_w[l], lnv_b=lnv_b[l], w_s=w_s[l], b_s=b_s[l], b_gate=b_gate[l],
                    w_branch=w_branch[l], w_out=w_out[l], ln_mlp=ln_mlp[l], w_up=w_up[l],
                    w_down=w_down[l])

    nb_prompt = x_prompt.shape[0]
    xp, xs = x_prompt, x_sample
    p_wkv, p_shift, p_kv, s_wkv, s_shift, s_kv, s_v = [], [], [], [], [], [], []
    for l in range(DEPTH):
        lp = layer_params(l)
        xp, sh, wk, kv, _ = _block(xp, lp, jnp.zeros((nb_prompt, SHIFT_W), xp.dtype),
                                   jnp.zeros((nb_prompt, H_A, HEAD_A, HEAD_A), jnp.float32), None)
        p_shift.append(sh)
        p_wkv.append(wk)
        p_kv.append(kv)
        xs, sh, wk, kv, vr = _block(xs, lp, state_shift[l], state_wkv[l],
                                    (cache_kv_w128[l], cache_kv_w512[l], cache_kv_w2048[l]))
        s_shift.append(sh)
        s_wkv.append(wk)
        s_kv.append(kv)
        s_v.append(vr)
    y_prompt = _rmsnorm(xp, ln_final)
    y_sample = _rmsnorm(xs, ln_final)
    p_wkv_a = jnp.stack(p_wkv)
    p_shift_a = jnp.stack(p_shift)
    p_kv128 = jnp.stack([kv[0] for kv in p_kv])
    p_kv512 = jnp.stack([kv[1] for kv in p_kv])
    p_kv2048 = jnp.stack([kv[2] for kv in p_kv])
    s_wkv_a = jnp.stack(s_wkv)
    s_shift_a = jnp.stack(s_shift)
    s_kv128 = jnp.stack([kv[0] for kv in s_kv])
    s_kv512 = jnp.stack([kv[1] for kv in s_kv])
    s_kv2048 = jnp.stack([kv[2] for kv in s_kv])
    s_gmlp_v = jnp.stack(s_v)
    return (y_prompt, y_sample, p_wkv_a, p_shift_a, p_kv128, p_kv512, p_kv2048,
            s_wkv_a, s_shift_a, s_kv128, s_kv512, s_kv2048, s_gmlp_v)
```

```python
import functools

import jax
import jax.numpy as jnp
from jax import lax
from jax.experimental import pallas as pl
from jax.experimental.pallas import tpu as pltpu

F32 = jnp.float32
BF16 = jnp.bfloat16

HEAD = 64
LORA_W, LORA_A, LORA_G = 64, 64, 160
GMLP_CHUNK = 128
GMLP_GROUPS = 8
WINDOWS = (128, 512, 2048)
DILATIONS = (1, 4, 16)
N_BACK = 128
ATTN_BLOCK = 128
NORM_EPS = 1e-6
LN_EPS = 1e-5
GN_EPS = 64e-5
MASK_VALUE = -1e30

LANES = 128
VMEM_LIMIT_BYTES = 48 * 2**20

LORA_PAD = (128, 128, 256)
WKV_CHUNK = 64

NN = ((1,), (0,))
NT = ((1,), (1,))
TN = ((0,), (0,))


def _cparams(*sem):
    return pltpu.CompilerParams(dimension_semantics=sem, vmem_limit_bytes=VMEM_LIMIT_BYTES)


def _dot(a, b, dims=NN):
    return lax.dot_general(a.astype(BF16), b.astype(BF16), (dims, ((), ())),
                           preferred_element_type=F32)


def _split(x):
    hi = x.astype(BF16)
    return hi, (x - hi.astype(F32)).astype(BF16)


def _dot3(a, b, dims=NN):
    ah, al = _split(a)
    bh, bl = _split(b)
    dg = lambda x, y: lax.dot_general(x, y, (dims, ((), ())), preferred_element_type=F32)
    return dg(ah, bh) + (dg(ah, bl) + dg(al, bh))


def _rms_matmul_body(x_ref, g_ref, w_ref, o_ref, xn_ref):
    @pl.when(pl.program_id(1) == 0)
    def _():
        x = x_ref[...]
        ms = jnp.mean(x * x, axis=-1, keepdims=True)
        xn_ref[...] = (x * lax.rsqrt(ms + NORM_EPS) * g_ref[...]).astype(BF16)

    o_ref[...] = jnp.dot(xn_ref[...], w_ref[...], preferred_element_type=F32)


def _rms_matmul(x, g, w, tm, tn):
    m, d = x.shape
    n = w.shape[1]
    return pl.pallas_call(
        _rms_matmul_body,
        grid=(m // tm, n // tn),
        in_specs=[pl.BlockSpec((tm, d), lambda i, j: (i, 0)),
                  pl.BlockSpec((1, d), lambda i, j: (0, 0)),
                  pl.BlockSpec((d, tn), lambda i, j: (0, j))],
        out_specs=pl.BlockSpec((tm, tn), lambda i, j: (i, j)),
        out_shape=jax.ShapeDtypeStruct((m, n), F32),
        scratch_shapes=[pltpu.VMEM((tm, d), BF16)],
        compiler_params=_cparams("parallel", "arbitrary"),
        name="rms_matmul",
    )(x, g, w)


def _rwkv_prep_body(z_ref, sp_ref, mu_ref, w0_ref, dw2_ref, a0_ref, a2_ref, g2_ref, kk_ref, ka_ref,
                    r_ref, lw_ref, k2_ref, v_ref, kku_ref, a_ref, g_ref, carry_ref, *, d):
    @pl.when(pl.program_id(1) == 0)
    def _():
        carry_ref[...] = sp_ref[...]

    zr = z_ref[...]
    tt = zr.shape[0]
    row = lax.broadcasted_iota(jnp.int32, zr.shape, 0)
    prev = jnp.where(row == 0, carry_ref[...], pltpu.roll(zr, 1, 0))
    carry_ref[...] = zr[tt - 1:tt, :]
    h = zr + (prev - zr) * mu_ref[...]
    e_w = 3 * d
    e_a = e_w + LORA_PAD[0]
    e_g = e_a + LORA_PAD[1]
    k = h[:, d:2 * d]
    xw = w0_ref[...] + _dot(jnp.tanh(h[:, e_w:e_a]), dw2_ref[...])
    softplus = jnp.maximum(-xw, 0.0) + jnp.log1p(jnp.exp(-jnp.abs(xw)))
    a = jax.nn.sigmoid(a0_ref[...] + _dot(h[:, e_a:e_g], a2_ref[...]))
    r_ref[...] = h[:, :d]
    lw_ref[...] = -jnp.exp(-softplus - 0.5)
    k2_ref[...] = k * (1.0 + (a - 1.0) * ka_ref[...])
    v_ref[...] = h[:, 2 * d:3 * d]
    kku_ref[...] = k * kk_ref[...]
    a_ref[...] = a
    g_ref[...] = _dot(jax.nn.sigmoid(h[:, e_g:]), g2_ref[...])


def _rwkv_prep(z_all, col_block, shift_prev, lp, tt):
    b, t, _ = z_all.shape
    d = lp['k_k'].shape[-1]
    wr = 3 * d + sum(LORA_PAD)
    full = lambda a: pl.BlockSpec(a.shape, lambda i, j: (0,) * a.ndim)
    params = [lp['mu_shift'], lp['decay_w0'], lp['decay_w2'], lp['iclr_a0'], lp['iclr_a2'],
              lp['gate_g2'], lp['k_k'], lp['k_a']]
    out_spec = pl.BlockSpec((None, tt, d), lambda i, j: (i, j, 0))
    return pl.pallas_call(
        functools.partial(_rwkv_prep_body, d=d),
        grid=(b, t // tt),
        in_specs=[pl.BlockSpec((None, tt, wr), lambda i, j: (i, j, col_block)),
                  pl.BlockSpec((None, 1, wr), lambda i, j: (i, 0, 0))] + [full(p) for p in params],
        out_specs=[out_spec] * 7,
        out_shape=[jax.ShapeDtypeStruct((b, t, d), F32)] * 7,
        scratch_shapes=[pltpu.VMEM((1, wr), F32)],
        compiler_params=_cparams("parallel", "arbitrary"),
        name="rwkv_prep",
    )(z_all, shift_prev, *params)


def _unit_lower_solver(a):
    c = a.shape[0]
    bs = min(c, 16)
    nb = c // bs
    assert nb in (1, 2, 4)
    ri = lax.broadcasted_iota(jnp.int32, (c, c), 0)
    ci = lax.broadcasted_iota(jnp.int32, (c, c), 1)
    same = (ri // bs) == (ci // bs)
    dg = jnp.where(same, a, 0.0)
    tinv = jnp.where(ri == ci, 1.0, 0.0) - dg
    pw = dg
    n = 2
    while n < bs:
        pw = _dot3(pw, pw)
        tinv = tinv + _dot3(tinv, pw)
        n *= 2
    m = _dot3(tinv, jnp.where(same, 0.0, a)) if nb > 1 else None

    def solve(rhs):
        x = _dot3(tinv, rhs)
        if nb > 2:
            x = x + _dot3(m, _dot3(m, x))
        if nb > 1:
            x = x - _dot3(m, x)
        return x

    return solve


def _wkv_chunk_body(r_ref, lw_ref, k2_ref, v_ref, kku_ref, a_ref, g_ref, lnw_ref, lnb_ref, rk_ref,
                    s0_ref, o_ref, s_ref, *, hb):
    @pl.when(pl.program_id(2) == 0)
    def _():
        s_ref[...] = s0_ref[...]

    c = r_ref.shape[0]
    ri = lax.broadcasted_iota(jnp.int32, (c, c), 0)
    ci = lax.broadcasted_iota(jnp.int32, (c, c), 1)
    lower = ri >= ci
    strict = ri > ci
    eye = jnp.where(ri == ci, 1.0, 0.0)
    ek = jnp.where(lax.broadcasted_iota(jnp.int32, (HEAD, HEAD), 0)
                   == lax.broadcasted_iota(jnp.int32, (HEAD, HEAD), 1), 1.0, 0.0)

    lw = lw_ref[...]
    cum = _dot3(jnp.where(lower, 1.0, 0.0), lw)
    gam = jnp.exp(cum)
    gam_prev = jnp.exp(cum - lw)
    igam = jnp.exp(-cum)
    tail = jnp.exp(cum[c - 1:c, :] - cum)

    for h in range(hb):
        sl = slice(HEAD * h, HEAD * (h + 1))
        r, k2, v = r_ref[:, sl], k2_ref[:, sl], v_ref[:, sl]
        kku = kku_ref[:, sl]
        kk = kku / jnp.maximum(jnp.sqrt(jnp.sum(kku * kku, axis=-1, keepdims=True)), 1e-12)
        bvec = kk * a_ref[:, sl]
        kt = kk * gam_prev[:, sl]
        rt = r * gam[:, sl]
        kh = k2 * igam[:, sl]
        bh = bvec * igam[:, sl]
        kb = k2 * tail[:, sl]
        bb = bvec * tail[:, sl]

        lhs = jnp.concatenate([kt, rt], axis=0)
        sb = _dot3(lhs, bh, NT)
        sk = _dot3(lhs, kh, NT)
        a_ub = jnp.where(strict, sb[:c], 0.0)
        a_vk = jnp.where(strict, sk[:c], 0.0)
        b_ub = jnp.where(lower, sb[c:], 0.0)
        b_vk = jnp.where(lower, sk[c:], 0.0)

        solve = _unit_lower_solver(a_ub)
        w_ = solve(kt)
        u0 = solve(_dot3(a_vk, v))
        q = rt - _dot3(b_ub, w_)
        y0 = _dot3(b_vk, v) - _dot3(b_ub, u0)
        s_in = s_ref[h]
        y = _dot3(q, s_in, NT) + y0
        p = ek * gam[c - 1:c, sl] - _dot3(w_, bb, TN)
        z = _dot3(v, kb, TN) - _dot3(u0, bb, TN)
        s_ref[h] = _dot3(s_in, p) + z

        mu = jnp.mean(y, axis=-1, keepdims=True)
        yc = y - mu
        var = jnp.mean(yc * yc, axis=-1, keepdims=True)
        yn = yc * lax.rsqrt(var + GN_EPS) * lnw_ref[:, sl] + lnb_ref[:, sl]
        bonus = jnp.sum(r * k2 * rk_ref[:, sl], axis=-1, keepdims=True) * v
        o_ref[:, sl] = (yn + bonus) * g_ref[:, sl]


def _wkv_chunk(parts, lp, s0, chunk, hb):
    b, t, d = parts[0].shape
    nh = d // HEAD
    wb = hb * HEAD
    tok = pl.BlockSpec((None, chunk, wb), lambda i, j, c: (i, c, j))
    par = pl.BlockSpec((1, wb), lambda i, j, c: (0, j))
    st = pl.BlockSpec((None, hb, HEAD, HEAD), lambda i, j, c: (i, j, 0, 0))
    return pl.pallas_call(
        functools.partial(_wkv_chunk_body, hb=hb),
        grid=(b, nh // hb, t // chunk),
        in_specs=[tok] * 7 + [par] * 3 + [st],
        out_specs=[tok, st],
        out_shape=[jax.ShapeDtypeStruct((b, t, d), F32),
                   jax.ShapeDtypeStruct((b, nh, HEAD, HEAD), F32)],
        compiler_params=_cparams("parallel", "parallel", "arbitrary"),
        name="wkv_chunk",
    )(*parts, lp['lnx_w'], lp['lnx_b'], lp['r_k'], s0)


def _gmlp_body(z_ref, lnw_ref, lnb_ref, ws_ref, bs_ref, o_ref, *vn_out, d):
    z = z_ref[...]
    z = 0.5 * z * (1.0 + lax.erf(z * (2.0 ** -0.5)))
    u, vv = z[:, :d], z[:, d:]
    mu = jnp.mean(vv, axis=-1, keepdims=True)
    vc = vv - mu
    var = jnp.mean(vc * vc, axis=-1, keepdims=True)
    vn = vc * lax.rsqrt(var + LN_EPS) * lnw_ref[...] + lnb_ref[...]
    if vn_out:
        vn_out[0][...] = vn
    c = z.shape[0]
    causal = (lax.broadcasted_iota(jnp.int32, (c, c), 0) >= lax.broadcasted_iota(jnp.int32, (c, c), 1))
    cg = d // GMLP_GROUPS
    for g in range(GMLP_GROUPS):
        sl = slice(cg * g, cg * (g + 1))
        mixed = _dot(jnp.where(causal, ws_ref[g], 0.0), vn[:, sl])
        o_ref[:, sl] = u[:, sl] * (mixed + bs_ref[:, sl])


def _gmlp(z_all, col_block, lp, chunk, want_vn):
    b, t, _ = z_all.shape
    d = lp['lnv_w'].shape[-1]
    ws, bs = lp['w_s'][:, :chunk, :chunk], lp['b_s_cols'][:chunk]
    full = lambda a: pl.BlockSpec(a.shape, lambda i, j: (0,) * a.ndim)
    out_spec = pl.BlockSpec((None, chunk, d), lambda i, j: (i, j, 0))
    n_out = 2 if want_vn else 1
    return pl.pallas_call(
        functools.partial(_gmlp_body, d=d),
        grid=(b, t // chunk),
        in_specs=[pl.BlockSpec((None, chunk, 2 * d), lambda i, j: (i, j, col_block)),
                  full(lp['lnv_w']), full(lp['lnv_b']), full(ws), full(bs)],
        out_specs=[out_spec] * n_out,
        out_shape=[jax.ShapeDtypeStruct((b, t, d), F32)] * n_out,
        compiler_params=_cparams("parallel", "parallel"),
        name="gmlp",
    )(z_all, lp['lnv_w'], lp['lnv_b'], ws, bs)


def _attn_group(slope_ref, q_ref, k_ref, v_ref, o_ref, m_ref, l_ref, dil):
    t = q_ref.shape[0]
    qb = ATTN_BLOCK
    nbk = t // (dil * qb)
    lane = lax.broadcasted_iota(jnp.int32, (qb, LANES), 1)
    first = lane < HEAD
    qi = lax.broadcasted_iota(jnp.int32, (qb, 2 * qb), 0)
    kj = lax.broadcasted_iota(jnp.int32, (qb, 2 * qb), 1)

    def rows(start, size):
        return pl.ds(start, size) if dil == 1 else pl.ds(start, size, stride=dil)

    def block(i, carry):
        res = i // nbk
        n = i % nbk
        n_lo = jnp.maximum(n - 1, 0)
        q_rows = rows(res + n * (qb * dil), qb)
        k_rows = rows(res + n_lo * (qb * dil), 2 * qb)
        q = q_ref[q_rows, :]
        k = k_ref[k_rows, :].astype(BF16)
        v = v_ref[k_rows, :].astype(BF16)
        acc, m_prev, l_prev = o_ref[q_rows, :], m_ref[q_rows, :], l_ref[q_rows, :]
        rel = (n - n_lo) * qb + qi - kj
        valid = (rel >= 0) & (rel <= N_BACK)
        dist = (rel * dil).astype(F32)
        halves = []
        for hh in range(2):
            c0 = HEAD * hh
            mine = first if hh == 0 else jnp.logical_not(first)
            s = _dot(jnp.where(mine, q, 0.0), k, NT) * (HEAD ** -0.5)
            s = jnp.where(valid, s - slope_ref[:, c0:c0 + 1] * dist, MASK_VALUE)
            mp = m_prev[:, c0:c0 + 1]
            mn = jnp.maximum(mp, jnp.max(s, axis=-1, keepdims=True))
            alpha = jnp.exp(mp - mn)
            p = jnp.exp(s - mn)
            ln = alpha * l_prev[:, c0:c0 + 1] + jnp.sum(p, axis=-1, keepdims=True)
            halves.append((alpha, mn, ln, _dot(p, v)))
        pick = lambda j: jnp.where(first, halves[0][j], halves[1][j])
        o_ref[q_rows, :] = acc * pick(0) + pick(3)
        m_ref[q_rows, :] = pick(1)
        l_ref[q_rows, :] = pick(2)
        return carry

    lax.fori_loop(0, dil * nbk, block, 0)


def _attn_prompt_body(slope_ref, q_ref, k_ref, v_ref, o_ref, m_ref, l_ref):
    g = pl.program_id(2)

    @pl.when(g == 0)
    def _():
        o_ref[...] = jnp.zeros_like(o_ref)
        m_ref[...] = jnp.full_like(m_ref, MASK_VALUE)
        l_ref[...] = jnp.zeros_like(l_ref)

    for gi, dil in enumerate(DILATIONS):
        @pl.when(g == gi)
        def _(dil=dil):
            _attn_group(slope_ref, q_ref, k_ref, v_ref, o_ref, m_ref, l_ref, dil)

    @pl.when(g == len(DILATIONS) - 1)
    def _():
        o_ref[...] = o_ref[...] / l_ref[...]


def _attn_prompt(z_all, col_off, d, slopes):
    b, t, _ = z_all.shape
    assert t % (max(DILATIONS) * 2 * ATTN_BLOCK) == 0
    npair = d // LANES
    base = col_off // LANES
    spec = lambda j: pl.BlockSpec((None, t, LANES),
                                  lambda i, p, g: (i, 0, base + (3 * g + j) * npair + p))
    return pl.pallas_call(
        _attn_prompt_body,
        grid=(b, npair, len(DILATIONS)),
        in_specs=[pl.BlockSpec((None, 1, LANES), lambda i, p, g: (p, 0, 0)), spec(0), spec(1), spec(2)],
        out_specs=pl.BlockSpec((None, t, LANES), lambda i, p, g: (i, 0, p)),
        out_shape=jax.ShapeDtypeStruct((b, t, d), F32),
        scratch_shapes=[pltpu.VMEM((t, LANES), F32), pltpu.VMEM((t, LANES), F32)],
        compiler_params=_cparams("parallel", "parallel", "arbitrary"),
        name="attn_prompt",
    )(slopes, z_all, z_all, z_all)


def _attn_sample_body(slope_ref, qkv_ref, c0_ref, c1_ref, c2_ref, o_ref):
    caches = (c0_ref, c1_ref, c2_ref)
    s_len = qkv_ref.shape[0]
    slope = slope_ref[...][:, :, 0:1]
    scale = HEAD ** -0.5
    ic = lax.broadcasted_iota(jnp.int32, (N_BACK, 1, 1), 0)
    inew = lax.broadcasted_iota(jnp.int32, (s_len, 1, 1), 0)
    for s in range(s_len):
        parts = []
        for gi, dil in enumerate(DILATIONS):
            cache = caches[gi]
            q = qkv_ref[s, gi, 0]
            res, hop = s % dil, s // dil
            kc = cache[0, :, res]
            vc = cache[1, :, res]
            kn = qkv_ref[:, gi, 1]
            vn = qkv_ref[:, gi, 2]
            steps_c = hop + N_BACK - ic
            sc = jnp.sum(kc * q[None], axis=-1, keepdims=True) * scale
            sc = jnp.where(steps_c <= N_BACK, sc - slope * (steps_c * dil).astype(F32), MASK_VALUE)
            gap = s - inew
            ok = (gap >= 0) & (gap % dil == 0) & (gap <= N_BACK * dil)
            sn = jnp.sum(kn * q[None], axis=-1, keepdims=True) * scale
            sn = jnp.where(ok, sn - slope * gap.astype(F32), MASK_VALUE)
            m = jnp.maximum(jnp.max(sc, axis=0, keepdims=True), jnp.max(sn, axis=0, keepdims=True))
            pc = jnp.exp(sc - m)
            pn = jnp.exp(sn - m)
            den = jnp.sum(pc, axis=0, keepdims=True) + jnp.sum(pn, axis=0, keepdims=True)
            num = jnp.sum(pc * vc, axis=0, keepdims=True) + jnp.sum(pn * vn, axis=0, keepdims=True)
            parts.append((m, den, num))
        m_all = functools.reduce(jnp.maximum, [p[0] for p in parts])
        den = sum(p[1] * jnp.exp(p[0] - m_all) for p in parts)
        num = sum(p[2] * jnp.exp(p[0] - m_all) for p in parts)
        o_ref[s] = (num / den)[0]


def _attn_sample(qkv, caches, slopes, hb):
    b, s_len, _, _, nh, _ = qkv.shape
    cache_specs = []
    for cache, dil in zip(caches, DILATIONS):
        nres = min(dil, s_len)
        assert cache.shape[2] == N_BACK and dil % nres == 0
        cache_specs.append(pl.BlockSpec((None, 2, N_BACK, nres, hb, HEAD),
                                        lambda i, j: (i, 0, 0, 0, j, 0)))
    return pl.pallas_call(
        _attn_sample_body,
        grid=(b, nh // hb),
        in_specs=[pl.BlockSpec((1, hb, LANES), lambda i, j: (0, j, 0)),
                  pl.BlockSpec((None, s_len, 3, 3, hb, HEAD), lambda i, j: (i, 0, 0, 0, j, 0))]
                 + cache_specs,
        out_specs=pl.BlockSpec((None, s_len, hb, HEAD), lambda i, j: (i, 0, j, 0)),
        out_shape=jax.ShapeDtypeStruct((b, s_len, nh, HEAD), F32),
        compiler_params=_cparams("parallel", "parallel"),
        name="attn_sample",
    )(slopes, qkv, *caches)


def _merge_body(x_ref, oa_ref, ob_ref, oc_ref, zg_ref, bg_ref, wb_ref, wo_ref, o_ref, *, d):
    acc = None
    for n, br in enumerate((oa_ref, ob_ref, oc_ref)):
        sl = slice(n * d, (n + 1) * d)
        term = _dot(br[...], wb_ref[n]) * jax.nn.sigmoid(zg_ref[:, sl] + bg_ref[:, sl])
        acc = term if acc is None else acc + term
    o_ref[...] = x_ref[...] + _dot(acc, wo_ref[...])


def _merge(x, oa, ob, oc, z_all, gate_block, lp, tm):
    m, d = x.shape
    row = pl.BlockSpec((tm, d), lambda i: (i, 0))
    full = lambda a: pl.BlockSpec(a.shape, lambda i: (0,) * a.ndim)
    return pl.pallas_call(
        functools.partial(_merge_body, d=d),
        grid=(m // tm,),
        in_specs=[row, row, row, row, pl.BlockSpec((tm, 3 * d), lambda i: (i, gate_block)),
                  full(lp['b_gate']), full(lp['w_branch']), full(lp['w_out'])],
        out_specs=row,
        out_shape=jax.ShapeDtypeStruct((m, d), F32),
        compiler_params=_cparams("parallel"),
        name="merge",
    )(x, oa, ob, oc, z_all, lp['b_gate'], lp['w_branch'], lp['w_out'])


def _mlp_body(x_ref, g_ref, wu_ref, wd_ref, gf_ref, o_ref, xn_ref, acc_ref, *, final_norm):
    j = pl.program_id(1)

    @pl.when(j == 0)
    def _():
        x = x_ref[...]
        ms = jnp.mean(x * x, axis=-1, keepdims=True)
        xn_ref[...] = (x * lax.rsqrt(ms + NORM_EPS) * g_ref[...]).astype(BF16)
        acc_ref[...] = jnp.zeros_like(acc_ref)

    h = jnp.maximum(jnp.dot(xn_ref[...], wu_ref[...], preferred_element_type=F32), 0.0)
    acc_ref[...] += _dot(h * h, wd_ref[...])

    @pl.when(j == pl.num_programs(1) - 1)
    def _():
        y = x_ref[...] + acc_ref[...]
        if final_norm:
            ms = jnp.mean(y * y, axis=-1, keepdims=True)
            y = y * lax.rsqrt(ms + NORM_EPS) * gf_ref[...]
        o_ref[...] = y


def _mlp(x, lp, g_final, final_norm, tm, tf):
    m, d = x.shape
    ff = lp['w_up'].shape[1]
    vec = pl.BlockSpec((1, d), lambda i, j: (0, 0))
    row = pl.BlockSpec((tm, d), lambda i, j: (i, 0))
    return pl.pallas_call(
        functools.partial(_mlp_body, final_norm=final_norm),
        grid=(m // tm, ff // tf),
        in_specs=[row, vec, pl.BlockSpec((d, tf), lambda i, j: (0, j)),
                  pl.BlockSpec((tf, d), lambda i, j: (j, 0)), vec],
        out_specs=row,
        out_shape=jax.ShapeDtypeStruct((m, d), F32),
        scratch_shapes=[pltpu.VMEM((tm, d), BF16), pltpu.VMEM((tm, d), F32)],
        compiler_params=_cparams("parallel", "arbitrary"),
        name="mlp",
    )(x, lp['ln_mlp'], lp['w_up'], lp['w_down'], g_final)


def _pad_cols(a, width):
    return jnp.pad(a, [(0, 0)] * (a.ndim - 1) + [(0, width - a.shape[-1])])


def _pad_rows(a, height):
    return jnp.pad(a, [(0, height - a.shape[0]), (0, 0)])


def _rwkv_cols(a, d):
    e = 3 * d
    pieces = [a[..., :e], _pad_cols(a[..., e:e + LORA_W], LORA_PAD[0]),
              _pad_cols(a[..., e + LORA_W:e + LORA_W + LORA_A], LORA_PAD[1]),
              _pad_cols(a[..., e + LORA_W + LORA_A:], LORA_PAD[2])]
    return jnp.concatenate(pieces, axis=-1)


def _rwkv_cols_inverse(a, d):
    e = 3 * d
    o1 = e + LORA_PAD[0]
    o2 = o1 + LORA_PAD[1]
    return jnp.concatenate([a[..., :e], a[..., e:e + LORA_W], a[..., o1:o1 + LORA_A],
                            a[..., o2:o2 + LORA_G]], axis=-1)


def _layer_params(l, d, ln_mix, w_in, mu_shift, decay_w0, decay_w2, iclr_a0, iclr_a2, gate_g2, k_k, k_a,
                  r_k, lnx_w, lnx_b, lnv_w, lnv_b, w_s, b_s, b_gate, w_branch, w_out, ln_mlp, w_up, w_down):
    shift_w = 3 * d + LORA_W + LORA_A + LORA_G
    e1 = shift_w
    e2 = e1 + 2 * d
    e3 = e2 + 9 * d
    w = w_in[l]
    w_perm = jnp.concatenate([w[:, e2:e3], w[:, e3:], w[:, e1:e2], _rwkv_cols(w[:, :e1], d)], axis=1)
    row = lambda a: a[l].reshape(1, -1)
    return dict(
        ln_mix=row(ln_mix), w_in=w_perm.astype(BF16), mu_shift=_rwkv_cols(row(mu_shift), d),
        decay_w0=row(decay_w0), decay_w2=_pad_rows(decay_w2[l], LORA_PAD[0]).astype(BF16),
        iclr_a0=row(iclr_a0), iclr_a2=_pad_rows(iclr_a2[l], LORA_PAD[1]).astype(BF16),
        gate_g2=_pad_rows(gate_g2[l], LORA_PAD[2]).astype(BF16),
        k_k=row(k_k), k_a=row(k_a), r_k=row(r_k), lnx_w=row(lnx_w), lnx_b=row(lnx_b),
        lnv_w=row(lnv_w), lnv_b=row(lnv_b), w_s=w_s[l],
        b_s_cols=jnp.repeat(b_s[l].T, d // GMLP_GROUPS, axis=1),
        b_gate=row(b_gate), w_branch=w_branch[l].astype(BF16), w_out=w_out[l].astype(BF16),
        ln_mlp=row(ln_mlp), w_up=w_up[l].astype(BF16), w_down=w_down[l].astype(BF16))


def _tile(n, pref):
    while n % pref:
        pref //= 2
    return pref


def _layer(x, lp, g_final, final_norm, shift_prev, wkv_prev, kv_prev, slopes_pair, slopes_head):
    b, t, d = x.shape
    m = b * t
    wr = 3 * d + sum(LORA_PAD)
    off_gate, off_gmlp, off_rwkv = 9 * d, 12 * d, 14 * d
    x2 = x.reshape(m, d)
    z_all = _rms_matmul(x2, lp['ln_mix'], lp['w_in'], _tile(m, 1024), 512).reshape(b, t, -1)

    sp = _rwkv_cols(shift_prev, d)[:, None, :]
    parts = _rwkv_prep(z_all, off_rwkv // wr, sp, lp, _tile(t, 128))
    o_a, wkv = _wkv_chunk(parts, lp, wkv_prev, _tile(t, WKV_CHUNK), 4)
    shift_last = _rwkv_cols_inverse(z_all[:, t - 1, off_rwkv:], d)

    zq = z_all[:, :, :9 * d].reshape(b, t, 3, 3, d // HEAD, HEAD)
    if kv_prev is None:
        (o_b,), v_rows = _gmlp(z_all, off_gmlp // (2 * d), lp, GMLP_CHUNK, False), None
        o_c = _attn_prompt(z_all, 0, d, slopes_pair)
        kv_new = []
        for gi, win in enumerate(WINDOWS):
            keep = min(win, t)
            kv_new.append(jnp.stack([zq[:, t - keep:, gi, 1], zq[:, t - keep:, gi, 2]], axis=1))
    else:
        o_b, v_rows = _gmlp(z_all, off_gmlp // (2 * d), lp, t, True)
        caches = [c.reshape(c.shape[0], 2, N_BACK, dil, d // HEAD, HEAD)
                  for c, dil in zip(kv_prev, DILATIONS)]
        o_c = _attn_sample(zq, caches, slopes_head, 8).reshape(b, t, d)
        kv_new = [jnp.concatenate([c[:, :, t:], jnp.stack([zq[:, :, gi, 1], zq[:, :, gi, 2]], axis=1)],
                                  axis=2) for gi, c in enumerate(kv_prev)]

    hmid = _merge(x2, o_a.reshape(m, d), o_b.reshape(m, d), o_c.reshape(m, d),
                  z_all.reshape(m, -1), off_gate // (3 * d), lp, _tile(m, 256))
    out = _mlp(hmid, lp, g_final, final_norm, _tile(m, 512), 1024)
    return out.reshape(b, t, d), shift_last, wkv, kv_new, v_rows


def kernel(x_prompt, x_sample, state_wkv, state_shift, cache_kv_w128, cache_kv_w512, cache_kv_w2048,
           ln_mix, w_in, mu_shift, decay_w0, decay_w2, iclr_a0, iclr_a2, gate_g2, k_k, k_a, r_k,
           lnx_w, lnx_b, lnv_w, lnv_b, w_s, b_s, b_gate, w_branch, w_out, ln_mlp, w_up, w_down,
           ln_final):
    depth = w_in.shape[0]
    d = x_prompt.shape[-1]
    nh = d // HEAD
    nb_prompt = x_prompt.shape[0]
    shift_w = state_shift.shape[-1]
    slopes = jnp.exp2(-8.0 * jnp.arange(1, nh + 1, dtype=F32) / nh)
    slopes_pair = jnp.repeat(slopes, HEAD).reshape(nh // 2, 1, LANES)
    slopes_head = jnp.broadcast_to(slopes[None, :, None], (1, nh, LANES))
    g_final = ln_final.reshape(1, d)
    caches = (cache_kv_w128, cache_kv_w512, cache_kv_w2048)

    xp, xs = x_prompt, x_sample
    outs_p, outs_s = [], []
    for l in range(depth):
        lp = _layer_params(l, d, ln_mix, w_in, mu_shift, decay_w0, decay_w2, iclr_a0, iclr_a2, gate_g2,
                           k_k, k_a, r_k, lnx_w, lnx_b, lnv_w, lnv_b, w_s, b_s, b_gate, w_branch, w_out,
                           ln_mlp, w_up, w_down)
        last = l == depth - 1
        xp, sh, wk, kv, _ = _layer(xp, lp, g_final, last, jnp.zeros((nb_prompt, shift_w), F32),
                                   jnp.zeros((nb_prompt, nh, HEAD, HEAD), F32), None,
                                   slopes_pair, slopes_head)
        outs_p.append((wk, sh, kv))
        xs, sh, wk, kv, vr = _layer(xs, lp, g_final, last, state_shift[l], state_wkv[l],
                                    tuple(c[l] for c in caches), slopes_pair, slopes_head)
        outs_s.append((wk, sh, kv, vr))

    stack = lambda f, outs: jnp.stack([f(o) for o in outs])
    return (xp, xs,
            stack(lambda o: o[0], outs_p), stack(lambda o: o[1], outs_p),
            stack(lambda o: o[2][0], outs_p), stack(lambda o: o[2][1], outs_p),
            stack(lambda o: o[2][2], outs_p),
            stack(lambda o: o[0], outs_s), stack(lambda o: o[1], outs_s),
            stack(lambda o: o[2][0], outs_s), stack(lambda o: o[2][1], outs_s),
            stack(lambda o: o[2][2], outs_s), stack(lambda o: o[3], outs_s))
```

```python
import functools

import jax
import jax.numpy as jnp
from jax import lax
from jax.experimental import pallas as pl
from jax.experimental.pallas import tpu as pltpu

F32 = jnp.float32
BF16 = jnp.bfloat16

HEAD = 64
LORA_W, LORA_A, LORA_G = 64, 64, 160
GMLP_CHUNK = 128
GMLP_GROUPS = 8
WINDOWS = (128, 512, 2048)
DILATIONS = (1, 4, 16)
N_BACK = 128
ATTN_BLOCK = 128
ATTN_UNROLL = 4
NORM_EPS = 1e-6
LN_EPS = 1e-5
GN_EPS = 64e-5
MASK_VALUE = -1e30

LANES = 128
VMEM_LIMIT_BYTES = 48 * 2**20

LORA_PAD = (128, 128, 256)
WKV_CHUNK = 64

NN = ((1,), (0,))
NT = ((1,), (1,))
TN = ((0,), (0,))


def _cparams(*sem):
    return pltpu.CompilerParams(dimension_semantics=sem, vmem_limit_bytes=VMEM_LIMIT_BYTES)


def _dot(a, b, dims=NN):
    return lax.dot_general(a.astype(BF16), b.astype(BF16), (dims, ((), ())),
                           preferred_element_type=F32)


def _rms_matmul_body(x_ref, g_ref, w_ref, o_ref, xn_ref):
    @pl.when(pl.program_id(1) == 0)
    def _():
        x = x_ref[...]
        ms = jnp.mean(x * x, axis=-1, keepdims=True)
        xn_ref[...] = (x * lax.rsqrt(ms + NORM_EPS) * g_ref[...]).astype(BF16)

    o_ref[...] = jnp.dot(xn_ref[...], w_ref[...], preferred_element_type=F32)


def _rms_matmul(x, g, w, tm, tn):
    m, d = x.shape
    n = w.shape[1]
    return pl.pallas_call(
        _rms_matmul_body,
        grid=(m // tm, n // tn),
        in_specs=[pl.BlockSpec((tm, d), lambda i, j: (i, 0)),
                  pl.BlockSpec((1, d), lambda i, j: (0, 0)),
                  pl.BlockSpec((d, tn), lambda i, j: (0, j))],
        out_specs=pl.BlockSpec((tm, tn), lambda i, j: (i, j)),
        out_shape=jax.ShapeDtypeStruct((m, n), F32),
        scratch_shapes=[pltpu.VMEM((tm, d), BF16)],
        compiler_params=_cparams("parallel", "arbitrary"),
        name="rms_matmul",
    )(x, g, w)


def _rwkv_prep_body(z_ref, sp_ref, mu_ref, w0_ref, dw2_ref, a0_ref, a2_ref, g2_ref, kk_ref, ka_ref,
                    r_ref, lw_ref, k2_ref, v_ref, kku_ref, a_ref, g_ref, carry_ref, *, d):
    @pl.when(pl.program_id(1) == 0)
    def _():
        carry_ref[...] = sp_ref[...]

    zr = z_ref[...]
    tt = zr.shape[0]
    row = lax.broadcasted_iota(jnp.int32, zr.shape, 0)
    prev = jnp.where(row == 0, carry_ref[...], pltpu.roll(zr, 1, 0))
    carry_ref[...] = zr[tt - 1:tt, :]
    h = zr + (prev - zr) * mu_ref[...]
    e_w = 3 * d
    e_a = e_w + LORA_PAD[0]
    e_g = e_a + LORA_PAD[1]
    k = h[:, d:2 * d]
    xw = w0_ref[...] + _dot(jnp.tanh(h[:, e_w:e_a]), dw2_ref[...])
    softplus = jnp.maximum(-xw, 0.0) + jnp.log1p(jnp.exp(-jnp.abs(xw)))
    a = jax.nn.sigmoid(a0_ref[...] + _dot(h[:, e_a:e_g], a2_ref[...]))
    r_ref[...] = h[:, :d]
    lw_ref[...] = -jnp.exp(-softplus - 0.5)
    k2_ref[...] = k * (1.0 + (a - 1.0) * ka_ref[...])
    v_ref[...] = h[:, 2 * d:3 * d]
    kku_ref[...] = k * kk_ref[...]
    a_ref[...] = a
    g_ref[...] = _dot(jax.nn.sigmoid(h[:, e_g:]), g2_ref[...])


def _rwkv_prep(z_all, col_block, shift_prev, lp, tt):
    b, t, _ = z_all.shape
    d = lp['k_k'].shape[-1]
    wr = 3 * d + sum(LORA_PAD)
    full = lambda a: pl.BlockSpec(a.shape, lambda i, j: (0,) * a.ndim)
    params = [lp['mu_shift'], lp['decay_w0'], lp['decay_w2'], lp['iclr_a0'], lp['iclr_a2'],
              lp['gate_g2'], lp['k_k'], lp['k_a']]
    out_spec = pl.BlockSpec((None, tt, d), lambda i, j: (i, j, 0))
    return pl.pallas_call(
        functools.partial(_rwkv_prep_body, d=d),
        grid=(b, t // tt),
        in_specs=[pl.BlockSpec((None, tt, wr), lambda i, j: (i, j, col_block)),
                  pl.BlockSpec((None, 1, wr), lambda i, j: (i, 0, 0))] + [full(p) for p in params],
        out_specs=[out_spec] * 7,
        out_shape=[jax.ShapeDtypeStruct((b, t, d), F32)] * 7,
        scratch_shapes=[pltpu.VMEM((1, wr), F32)],
        compiler_params=_cparams("parallel", "arbitrary"),
        name="rwkv_prep",
    )(z_all, shift_prev, *params)


def _each(f, *lists):
    return [f(*args) for args in zip(*lists)]


def _unit_lower_solvers(mats):
    c = mats[0].shape[0]
    bs = min(c, 16)
    nb = c // bs
    assert nb in (1, 2, 4)
    ri = lax.broadcasted_iota(jnp.int32, (c, c), 0)
    ci = lax.broadcasted_iota(jnp.int32, (c, c), 1)
    same = (ri // bs) == (ci // bs)
    eye = jnp.where(ri == ci, 1.0, 0.0)
    pw = [jnp.where(same, a, 0.0) for a in mats]
    tinv = [eye - d for d in pw]
    n = 2
    while n < bs:
        pw = _each(_dot, pw, pw)
        tinv = _each(lambda t, x: t + _dot(t, x), tinv, pw)
        n *= 2
    m = _each(lambda t, a: _dot(t, jnp.where(same, 0.0, a)), tinv, mats) if nb > 1 else None

    def solve(rhs):
        x = _each(_dot, tinv, rhs)
        if nb > 2:
            mx = _each(_dot, m, x)
            x = _each(lambda x_, m_, y_: x_ + _dot(m_, y_), x, m, mx)
        if nb > 1:
            x = _each(lambda x_, m_: x_ - _dot(m_, x_), x, m)
        return x

    return solve


def _cumsum_rows(x):
    c = x.shape[0]
    row = lax.broadcasted_iota(jnp.int32, x.shape, 0)
    s = 1
    while s < c:
        x = x + jnp.where(row >= s, pltpu.roll(x, s, 0), 0.0)
        s *= 2
    return x


def _wkv_chunk_body(r_ref, lw_ref, k2_ref, v_ref, kku_ref, a_ref, g_ref, lnw_ref, lnb_ref, rk_ref,
                    s0_ref, o_ref, s_ref, *, hb):
    @pl.when(pl.program_id(2) == 0)
    def _():
        s_ref[...] = s0_ref[...]

    c = r_ref.shape[0]
    ri = lax.broadcasted_iota(jnp.int32, (c, c), 0)
    ci = lax.broadcasted_iota(jnp.int32, (c, c), 1)
    lower = ri >= ci
    strict = ri > ci
    ek = jnp.where(lax.broadcasted_iota(jnp.int32, (HEAD, HEAD), 0)
                   == lax.broadcasted_iota(jnp.int32, (HEAD, HEAD), 1), 1.0, 0.0)

    lw = lw_ref[...]
    cum = _cumsum_rows(lw)
    gam = jnp.exp(cum)
    gam_prev = jnp.exp(cum - lw)
    igam = jnp.exp(-cum)
    tail = jnp.exp(cum[c - 1:c, :] - cum)

    sls = [slice(HEAD * h, HEAD * (h + 1)) for h in range(hb)]
    cols = lambda x: [x[:, sl] for sl in sls]
    r, k2, v, kku, av = cols(r_ref), cols(k2_ref), cols(v_ref), cols(kku_ref), cols(a_ref)
    kk = [x / jnp.maximum(jnp.sqrt(jnp.sum(x * x, axis=-1, keepdims=True)), 1e-12) for x in kku]
    mul = lambda x, y: x * y
    bvec = _each(mul, kk, av)
    kt = _each(mul, kk, cols(gam_prev))
    rt = _each(mul, r, cols(gam))
    kh = _each(mul, k2, cols(igam))
    bh = _each(mul, bvec, cols(igam))
    kb = _each(mul, k2, cols(tail))
    bb = _each(mul, bvec, cols(tail))

    lhs = _each(lambda x, y: jnp.concatenate([x, y], axis=0), kt, rt)
    sb = _each(lambda x, y: _dot(x, y, NT), lhs, bh)
    sk = _each(lambda x, y: _dot(x, y, NT), lhs, kh)
    a_ub = [jnp.where(strict, x[:c], 0.0) for x in sb]
    a_vk = [jnp.where(strict, x[:c], 0.0) for x in sk]
    b_ub = [jnp.where(lower, x[c:], 0.0) for x in sb]
    b_vk = [jnp.where(lower, x[c:], 0.0) for x in sk]

    solve = _unit_lower_solvers(a_ub)
    av_v = _each(_dot, a_vk, v)
    sol = solve(_each(lambda x, y: jnp.concatenate([x, y], axis=1), kt, av_v))
    bu = _each(_dot, b_ub, sol)
    bv_v = _each(_dot, b_vk, v)
    w_ = [x[:, :HEAD] for x in sol]
    u0 = [x[:, HEAD:] for x in sol]
    q = _each(lambda x, y: x - y[:, :HEAD], rt, bu)
    y0 = _each(lambda x, y: x - y[:, HEAD:], bv_v, bu)
    s_in = [s_ref[h] for h in range(hb)]
    y = _each(lambda q_, s_, y_: _dot(q_, s_, NT) + y_, q, s_in, y0)
    p = _each(lambda w, b_, sl: ek * gam[c - 1:c, sl] - _dot(w, b_, TN), w_, bb, sls)
    z = _each(lambda v_, u_, kb_, bb_: _dot(jnp.concatenate([v_, -u_], axis=0),
                                            jnp.concatenate([kb_, bb_], axis=0), TN), v, u0, kb, bb)
    s_new = _each(lambda s_, p_, z_: _dot(s_, p_) + z_, s_in, p, z)
    for h in range(hb):
        s_ref[h] = s_new[h]

    mu = [jnp.mean(x, axis=-1, keepdims=True) for x in y]
    yc = _each(lambda x, m_: x - m_, y, mu)
    var = [jnp.mean(x * x, axis=-1, keepdims=True) for x in yc]
    for h, sl in enumerate(sls):
        yn = yc[h] * lax.rsqrt(var[h] + GN_EPS) * lnw_ref[:, sl] + lnb_ref[:, sl]
        bonus = jnp.sum(r[h] * k2[h] * rk_ref[:, sl], axis=-1, keepdims=True) * v[h]
        o_ref[:, sl] = (yn + bonus) * g_ref[:, sl]


def _wkv_chunk(parts, lp, s0, chunk, hb):
    b, t, d = parts[0].shape
    nh = d // HEAD
    wb = hb * HEAD
    tok = pl.BlockSpec((None, chunk, wb), lambda i, j, c: (i, c, j))
    par = pl.BlockSpec((1, wb), lambda i, j, c: (0, j))
    st = pl.BlockSpec((None, hb, HEAD, HEAD), lambda i, j, c: (i, j, 0, 0))
    return pl.pallas_call(
        functools.partial(_wkv_chunk_body, hb=hb),
        grid=(b, nh // hb, t // chunk),
        in_specs=[tok] * 7 + [par] * 3 + [st],
        out_specs=[tok, st],
        out_shape=[jax.ShapeDtypeStruct((b, t, d), F32),
                   jax.ShapeDtypeStruct((b, nh, HEAD, HEAD), F32)],
        compiler_params=_cparams("parallel", "parallel", "arbitrary"),
        name="wkv_chunk",
    )(*parts, lp['lnx_w'], lp['lnx_b'], lp['r_k'], s0)


def _gmlp_body(z_ref, lnw_ref, lnb_ref, ws_ref, bs_ref, o_ref, *vn_out, d):
    z = z_ref[...]
    z = 0.5 * z * (1.0 + lax.erf(z * (2.0 ** -0.5)))
    u, vv = z[:, :d], z[:, d:]
    mu = jnp.mean(vv, axis=-1, keepdims=True)
    vc = vv - mu
    var = jnp.mean(vc * vc, axis=-1, keepdims=True)
    vn = vc * lax.rsqrt(var + LN_EPS) * lnw_ref[...] + lnb_ref[...]
    if vn_out:
        vn_out[0][...] = vn
    c = z.shape[0]
    causal = (lax.broadcasted_iota(jnp.int32, (c, c), 0) >= lax.broadcasted_iota(jnp.int32, (c, c), 1))
    cg = d // GMLP_GROUPS
    for g in range(GMLP_GROUPS):
        sl = slice(cg * g, cg * (g + 1))
        mixed = _dot(jnp.where(causal, ws_ref[g], 0.0), vn[:, sl])
        o_ref[:, sl] = u[:, sl] * (mixed + bs_ref[:, sl])


def _gmlp(z_all, col_block, lp, chunk, want_vn):
    b, t, _ = z_all.shape
    d = lp['lnv_w'].shape[-1]
    ws, bs = lp['w_s'][:, :chunk, :chunk], lp['b_s_cols'][:chunk]
    full = lambda a: pl.BlockSpec(a.shape, lambda i, j: (0,) * a.ndim)
    out_spec = pl.BlockSpec((None, chunk, d), lambda i, j: (i, j, 0))
    n_out = 2 if want_vn else 1
    return pl.pallas_call(
        functools.partial(_gmlp_body, d=d),
        grid=(b, t // chunk),
        in_specs=[pl.BlockSpec((None, chunk, 2 * d), lambda i, j: (i, j, col_block)),
                  full(lp['lnv_w']), full(lp['lnv_b']), full(ws), full(bs)],
        out_specs=[out_spec] * n_out,
        out_shape=[jax.ShapeDtypeStruct((b, t, d), F32)] * n_out,
        compiler_params=_cparams("parallel", "parallel"),
        name="gmlp",
    )(z_all, lp['lnv_w'], lp['lnv_b'], ws, bs)


def _attn_group(slope_ref, q_ref, k_ref, v_ref, o_ref, ls_ref, m_ref, dil):
    t = q_ref.shape[0]
    qb = ATTN_BLOCK
    nbk = t // (dil * qb)

    def own_lanes(hh, nrows):
        lane = lax.broadcasted_iota(jnp.int32, (nrows, LANES), 1)
        return lane < HEAD if hh == 0 else lane >= HEAD

    first = own_lanes(0, qb)
    mine = (first, own_lanes(1, qb))

    def rows(start, size):
        return pl.ds(start, size) if dil == 1 else pl.ds(start, size, stride=dil)

    def make_bias(span, shift):
        rel = (shift + lax.broadcasted_iota(jnp.int32, (qb, span), 0)
               - lax.broadcasted_iota(jnp.int32, (qb, span), 1))
        valid = (rel >= 0) & (rel <= N_BACK)
        dist = (rel * dil).astype(F32)
        return [jnp.where(valid, -slope_ref[:, HEAD * hh:HEAD * hh + 1] * dist, MASK_VALUE)
                for hh in range(2)]

    def update(blocks, bias):
        q = [q_ref[qr, :] * (HEAD ** -0.5) for qr, _ in blocks]
        k = [k_ref[kr, :].astype(BF16) for _, kr in blocks]
        v = [v_ref[kr, :] for _, kr in blocks]
        units = [(u, hh) for u in range(len(blocks)) for hh in range(2)]
        s = [_dot(jnp.where(mine[hh], q[u], 0.0), k[u], NT) + bias[hh] for u, hh in units]
        mp = [m_ref[hh, blocks[u][0], :] for u, hh in units]
        rm = [jnp.broadcast_to(jnp.max(x, axis=-1, keepdims=True), (qb, LANES)) for x in s]
        mn = _each(jnp.maximum, mp, rm)
        alpha = _each(lambda a, b: jnp.exp(a - b), mp, mn)
        p = _each(lambda x, m_: jnp.exp(x - jnp.concatenate([m_] * (x.shape[1] // LANES), axis=1)), s, mn)
        vo = [jnp.where(own_lanes(hh, v[u].shape[0]), v[u], 1.0).astype(BF16) for u, hh in units]
        pv = _each(_dot, p, vo)
        done = []
        for u, (qr, _) in enumerate(blocks):
            a0, a1, r0, r1 = alpha[2 * u], alpha[2 * u + 1], pv[2 * u], pv[2 * u + 1]
            acc = o_ref[qr, :] * jnp.where(first, a0, a1) + jnp.where(first, r0, r1)
            den = ls_ref[qr, :] * jnp.where(first, a1, a0) + jnp.where(first, r1, r0)
            done.append((qr, acc, den, mn[2 * u], mn[2 * u + 1]))
        for qr, acc, den, m0, m1 in done:
            o_ref[qr, :] = acc
            ls_ref[qr, :] = den
            m_ref[0, qr, :] = m0
            m_ref[1, qr, :] = m1

    def run(nblocks, rows_fn, bias):
        main = nblocks // ATTN_UNROLL

        def body(width, base):
            def trip(i, carry):
                update([rows_fn(base + i * width + u) for u in range(width)], bias)
                return carry
            return trip

        if main:
            lax.fori_loop(0, main, body(ATTN_UNROLL, 0), 0)
        if nblocks - main * ATTN_UNROLL:
            lax.fori_loop(0, nblocks - main * ATTN_UNROLL, body(1, main * ATTN_UNROLL), 0)

    run(dil, lambda i: (rows(i, qb), rows(i, qb)), make_bias(qb, 0))

    def later(i):
        res = i // (nbk - 1)
        n = 1 + i % (nbk - 1)
        return rows(res + n * (qb * dil), qb), rows(res + (n - 1) * (qb * dil), 2 * qb)

    run(dil * (nbk - 1), later, make_bias(2 * qb, qb))


def _attn_prompt_body(slope_ref, q_ref, k_ref, v_ref, o_ref, ls_ref, m_ref):
    g = pl.program_id(2)

    @pl.when(g == 0)
    def _():
        o_ref[...] = jnp.zeros_like(o_ref)
        ls_ref[...] = jnp.zeros_like(ls_ref)
        m_ref[...] = jnp.full_like(m_ref, MASK_VALUE)

    for gi, dil in enumerate(DILATIONS):
        @pl.when(g == gi)
        def _(dil=dil):
            _attn_group(slope_ref, q_ref, k_ref, v_ref, o_ref, ls_ref, m_ref, dil)

    @pl.when(g == len(DILATIONS) - 1)
    def _():
        o_ref[...] = o_ref[...] / pltpu.roll(ls_ref[...], HEAD, 1)


def _attn_prompt(z_all, col_off, d, slopes):
    b, t, _ = z_all.shape
    assert t % (max(DILATIONS) * 2 * ATTN_BLOCK) == 0
    npair = d // LANES
    base = col_off // LANES
    spec = lambda j: pl.BlockSpec((None, t, LANES),
                                  lambda i, p, g: (i, 0, base + (3 * g + j) * npair + p))
    return pl.pallas_call(
        _attn_prompt_body,
        grid=(b, npair, len(DILATIONS)),
        in_specs=[pl.BlockSpec((None, 1, LANES), lambda i, p, g: (p, 0, 0)), spec(0), spec(1), spec(2)],
        out_specs=pl.BlockSpec((None, t, LANES), lambda i, p, g: (i, 0, p)),
        out_shape=jax.ShapeDtypeStruct((b, t, d), F32),
        scratch_shapes=[pltpu.VMEM((t, LANES), F32), pltpu.VMEM((2, t, LANES), F32)],
        compiler_params=_cparams("parallel", "parallel", "arbitrary"),
        name="attn_prompt",
    )(slopes, z_all, z_all, z_all)


def _attn_sample_body(slope_ref, qkv_ref, c0_ref, c1_ref, c2_ref, o_ref):
    caches = (c0_ref, c1_ref, c2_ref)
    s_len = qkv_ref.shape[0]
    slope = slope_ref[...][:, :, 0:1]
    scale = HEAD ** -0.5
    ic = lax.broadcasted_iota(jnp.int32, (N_BACK, 1, 1), 0)
    inew = lax.broadcasted_iota(jnp.int32, (s_len, 1, 1), 0)
    for s in range(s_len):
        parts = []
        for gi, dil in enumerate(DILATIONS):
            cache = caches[gi]
            q = qkv_ref[s, gi, 0]
            res, hop = s % dil, s // dil
            kc = cache[0, :, res]
            vc = cache[1, :, res]
            kn = qkv_ref[:, gi, 1]
            vn = qkv_ref[:, gi, 2]
            steps_c = hop + N_BACK - ic
            sc = jnp.sum(kc * q[None], axis=-1, keepdims=True) * scale
            sc = jnp.where(steps_c <= N_BACK, sc - slope * (steps_c * dil).astype(F32), MASK_VALUE)
            gap = s - inew
            ok = (gap >= 0) & (gap % dil == 0) & (gap <= N_BACK * dil)
            sn = jnp.sum(kn * q[None], axis=-1, keepdims=True) * scale
            sn = jnp.where(ok, sn - slope * gap.astype(F32), MASK_VALUE)
            m = jnp.maximum(jnp.max(sc, axis=0, keepdims=True), jnp.max(sn, axis=0, keepdims=True))
            pc = jnp.exp(sc - m)
            pn = jnp.exp(sn - m)
            den = jnp.sum(pc, axis=0, keepdims=True) + jnp.sum(pn, axis=0, keepdims=True)
            num = jnp.sum(pc * vc, axis=0, keepdims=True) + jnp.sum(pn * vn, axis=0, keepdims=True)
            parts.append((m, den, num))
        m_all = functools.reduce(jnp.maximum, [p[0] for p in parts])
        den = sum(p[1] * jnp.exp(p[0] - m_all) for p in parts)
        num = sum(p[2] * jnp.exp(p[0] - m_all) for p in parts)
        o_ref[s] = (num / den)[0]


def _attn_sample(qkv, caches, slopes, hb):
    b, s_len, _, _, nh, _ = qkv.shape
    cache_specs = []
    for cache, dil in zip(caches, DILATIONS):
        nres = min(dil, s_len)
        assert cache.shape[2] == N_BACK and dil % nres == 0
        cache_specs.append(pl.BlockSpec((None, 2, N_BACK, nres, hb, HEAD),
                                        lambda i, j: (i, 0, 0, 0, j, 0)))
    return pl.pallas_call(
        _attn_sample_body,
        grid=(b, nh // hb),
        in_specs=[pl.BlockSpec((1, hb, LANES), lambda i, j: (0, j, 0)),
                  pl.BlockSpec((None, s_len, 3, 3, hb, HEAD), lambda i, j: (i, 0, 0, 0, j, 0))]
                 + cache_specs,
        out_specs=pl.BlockSpec((None, s_len, hb, HEAD), lambda i, j: (i, 0, j, 0)),
        out_shape=jax.ShapeDtypeStruct((b, s_len, nh, HEAD), F32),
        compiler_params=_cparams("parallel", "parallel"),
        name="attn_sample",
    )(slopes, qkv, *caches)


def _merge_body(x_ref, oa_ref, ob_ref, oc_ref, zg_ref, bg_ref, wb_ref, wo_ref, o_ref, *, d):
    acc = None
    for n, br in enumerate((oa_ref, ob_ref, oc_ref)):
        sl = slice(n * d, (n + 1) * d)
        term = _dot(br[...], wb_ref[n]) * jax.nn.sigmoid(zg_ref[:, sl] + bg_ref[:, sl])
        acc = term if acc is None else acc + term
    o_ref[...] = x_ref[...] + _dot(acc, wo_ref[...])


def _merge(x, oa, ob, oc, z_all, gate_block, lp, tm):
    m, d = x.shape
    row = pl.BlockSpec((tm, d), lambda i: (i, 0))
    full = lambda a: pl.BlockSpec(a.shape, lambda i: (0,) * a.ndim)
    return pl.pallas_call(
        functools.partial(_merge_body, d=d),
        grid=(m // tm,),
        in_specs=[row, row, row, row, pl.BlockSpec((tm, 3 * d), lambda i: (i, gate_block)),
                  full(lp['b_gate']), full(lp['w_branch']), full(lp['w_out'])],
        out_specs=row,
        out_shape=jax.ShapeDtypeStruct((m, d), F32),
        compiler_params=_cparams("parallel"),
        name="merge",
    )(x, oa, ob, oc, z_all, lp['b_gate'], lp['w_branch'], lp['w_out'])


def _mlp_body(x_ref, g_ref, wu_ref, wd_ref, gf_ref, o_ref, xn_ref, acc_ref, *, final_norm):
    j = pl.program_id(1)

    @pl.when(j == 0)
    def _():
        x = x_ref[...]
        ms = jnp.mean(x * x, axis=-1, keepdims=True)
        xn_ref[...] = (x * lax.rsqrt(ms + NORM_EPS) * g_ref[...]).astype(BF16)
        acc_ref[...] = jnp.zeros_like(acc_ref)

    h = jnp.maximum(jnp.dot(xn_ref[...], wu_ref[...], preferred_element_type=F32), 0.0)
    acc_ref[...] += _dot(h * h, wd_ref[...])

    @pl.when(j == pl.num_programs(1) - 1)
    def _():
        y = x_ref[...] + acc_ref[...]
        if final_norm:
            ms = jnp.mean(y * y, axis=-1, keepdims=True)
            y = y * lax.rsqrt(ms + NORM_EPS) * gf_ref[...]
        o_ref[...] = y


def _mlp(x, lp, g_final, final_norm, tm, tf):
    m, d = x.shape
    ff = lp['w_up'].shape[1]
    vec = pl.BlockSpec((1, d), lambda i, j: (0, 0))
    row = pl.BlockSpec((tm, d), lambda i, j: (i, 0))
    return pl.pallas_call(
        functools.partial(_mlp_body, final_norm=final_norm),
        grid=(m // tm, ff // tf),
        in_specs=[row, vec, pl.BlockSpec((d, tf), lambda i, j: (0, j)),
                  pl.BlockSpec((tf, d), lambda i, j: (j, 0)), vec],
        out_specs=row,
        out_shape=jax.ShapeDtypeStruct((m, d), F32),
        scratch_shapes=[pltpu.VMEM((tm, d), BF16), pltpu.VMEM((tm, d), F32)],
        compiler_params=_cparams("parallel", "arbitrary"),
        name="mlp",
    )(x, lp['ln_mlp'], lp['w_up'], lp['w_down'], g_final)


def _pad_cols(a, width):
    return jnp.pad(a, [(0, 0)] * (a.ndim - 1) + [(0, width - a.shape[-1])])


def _pad_rows(a, height):
    return jnp.pad(a, [(0, height - a.shape[0]), (0, 0)])


def _rwkv_cols(a, d):
    e = 3 * d
    pieces = [a[..., :e], _pad_cols(a[..., e:e + LORA_W], LORA_PAD[0]),
              _pad_cols(a[..., e + LORA_W:e + LORA_W + LORA_A], LORA_PAD[1]),
              _pad_cols(a[..., e + LORA_W + LORA_A:], LORA_PAD[2])]
    return jnp.concatenate(pieces, axis=-1)


def _rwkv_cols_inverse(a, d):
    e = 3 * d
    o1 = e + LORA_PAD[0]
    o2 = o1 + LORA_PAD[1]
    return jnp.concatenate([a[..., :e], a[..., e:e + LORA_W], a[..., o1:o1 + LORA_A],
                            a[..., o2:o2 + LORA_G]], axis=-1)


def _layer_params(l, d, ln_mix, w_in, mu_shift, decay_w0, decay_w2, iclr_a0, iclr_a2, gate_g2, k_k, k_a,
                  r_k, lnx_w, lnx_b, lnv_w, lnv_b, w_s, b_s, b_gate, w_branch, w_out, ln_mlp, w_up, w_down):
    shift_w = 3 * d + LORA_W + LORA_A + LORA_G
    e1 = shift_w
    e2 = e1 + 2 * d
    e3 = e2 + 9 * d
    w = w_in[l]
    w_perm = jnp.concatenate([w[:, e2:e3], w[:, e3:], w[:, e1:e2], _rwkv_cols(w[:, :e1], d)], axis=1)
    row = lambda a: a[l].reshape(1, -1)
    return dict(
        ln_mix=row(ln_mix), w_in=w_perm.astype(BF16), mu_shift=_rwkv_cols(row(mu_shift), d),
        decay_w0=row(decay_w0), decay_w2=_pad_rows(decay_w2[l], LORA_PAD[0]).astype(BF16),
        iclr_a0=row(iclr_a0), iclr_a2=_pad_rows(iclr_a2[l], LORA_PAD[1]).astype(BF16),
        gate_g2=_pad_rows(gate_g2[l], LORA_PAD[2]).astype(BF16),
        k_k=row(k_k), k_a=row(k_a), r_k=row(r_k), lnx_w=row(lnx_w), lnx_b=row(lnx_b),
        lnv_w=row(lnv_w), lnv_b=row(lnv_b), w_s=w_s[l],
        b_s_cols=jnp.repeat(b_s[l].T, d // GMLP_GROUPS, axis=1),
        b_gate=row(b_gate), w_branch=w_branch[l].astype(BF16), w_out=w_out[l].astype(BF16),
        ln_mlp=row(ln_mlp), w_up=w_up[l].astype(BF16), w_down=w_down[l].astype(BF16))


def _tile(n, pref):
    while n % pref:
        pref //= 2
    return pref


def _layer(x, lp, g_final, final_norm, shift_prev, wkv_prev, kv_prev, slopes_pair, slopes_head):
    b, t, d = x.shape
    m = b * t
    wr = 3 * d + sum(LORA_PAD)
    off_gate, off_gmlp, off_rwkv = 9 * d, 12 * d, 14 * d
    x2 = x.reshape(m, d)
    z_all = _rms_matmul(x2, lp['ln_mix'], lp['w_in'], _tile(m, 1024), 512).reshape(b, t, -1)

    sp = _rwkv_cols(shift_prev, d)[:, None, :]
    parts = _rwkv_prep(z_all, off_rwkv // wr, sp, lp, _tile(t, 128))
    o_a, wkv = _wkv_chunk(parts, lp, wkv_prev, _tile(t, WKV_CHUNK), 8)
    shift_last = _rwkv_cols_inverse(z_all[:, t - 1, off_rwkv:], d)

    zq = z_all[:, :, :9 * d].reshape(b, t, 3, 3, d // HEAD, HEAD)
    if kv_prev is None:
        (o_b,), v_rows = _gmlp(z_all, off_gmlp // (2 * d), lp, GMLP_CHUNK, False), None
        o_c = _attn_prompt(z_all, 0, d, slopes_pair)
        kv_new = []
        for gi, win in enumerate(WINDOWS):
            keep = min(win, t)
            kv_new.append(jnp.stack([zq[:, t - keep:, gi, 1], zq[:, t - keep:, gi, 2]], axis=1))
    else:
        o_b, v_rows = _gmlp(z_all, off_gmlp // (2 * d), lp, t, True)
        caches = [c.reshape(c.shape[0], 2, N_BACK, dil, d // HEAD, HEAD)
                  for c, dil in zip(kv_prev, DILATIONS)]
        o_c = _attn_sample(zq, caches, slopes_head, 8).reshape(b, t, d)
        kv_new = [jnp.concatenate([c[:, :, t:], jnp.stack([zq[:, :, gi, 1], zq[:, :, gi, 2]], axis=1)],
                                  axis=2) for gi, c in enumerate(kv_prev)]

    hmid = _merge(x2, o_a.reshape(m, d), o_b.reshape(m, d), o_c.reshape(m, d),
                  z_all.reshape(m, -1), off_gate // (3 * d), lp, _tile(m, 256))
    out = _mlp(hmid, lp, g_final, final_norm, _tile(m, 512), 1024)
    return out.reshape(b, t, d), shift_last, wkv, kv_new, v_rows


def kernel(x_prompt, x_sample, state_wkv, state_shift, cache_kv_w128, cache_kv_w512, cache_kv_w2048,
           ln_mix, w_in, mu_shift, decay_w0, decay_w2, iclr_a0, iclr_a2, gate_g2, k_k, k_a, r_k,
           lnx_w, lnx_b, lnv_w, lnv_b, w_s, b_s, b_gate, w_branch, w_out, ln_mlp, w_up, w_down,
           ln_final):
    depth = w_in.shape[0]
    d = x_prompt.shape[-1]
    nh = d // HEAD
    nb_prompt = x_prompt.shape[0]
    shift_w = state_shift.shape[-1]
    slopes = jnp.exp2(-8.0 * jnp.arange(1, nh + 1, dtype=F32) / nh)
    slopes_pair = jnp.repeat(slopes, HEAD).reshape(nh // 2, 1, LANES)
    slopes_head = jnp.broadcast_to(slopes[None, :, None], (1, nh, LANES))
    g_final = ln_final.reshape(1, d)
    caches = (cache_kv_w128, cache_kv_w512, cache_kv_w2048)

    xp, xs = x_prompt, x_sample
    outs_p, outs_s = [], []
    for l in range(depth):
        lp = _layer_params(l, d, ln_mix, w_in, mu_shift, decay_w0, decay_w2, iclr_a0, iclr_a2, gate_g2,
                           k_k, k_a, r_k, lnx_w, lnx_b, lnv_w, lnv_b, w_s, b_s, b_gate, w_branch, w_out,
                           ln_mlp, w_up, w_down)
        last = l == depth - 1
        xp, sh, wk, kv, _ = _layer(xp, lp, g_final, last, jnp.zeros((nb_prompt, shift_w), F32),
                                   jnp.zeros((nb_prompt, nh, HEAD, HEAD), F32), None,
                                   slopes_pair, slopes_head)
        outs_p.append((wk, sh, kv))
        xs, sh, wk, kv, vr = _layer(xs, lp, g_final, last, state_shift[l], state_wkv[l],
                                    tuple(c[l] for c in caches), slopes_pair, slopes_head)
        outs_s.append((wk, sh, kv, vr))

    stack = lambda f, outs: jnp.stack([f(o) for o in outs])
    return (xp, xs,
            stack(lambda o: o[0], outs_p), stack(lambda o: o[1], outs_p),
            stack(lambda o: o[2][0], outs_p), stack(lambda o: o[2][1], outs_p),
            stack(lambda o: o[2][2], outs_p),
            stack(lambda o: o[0], outs_s), stack(lambda o: o[1], outs_s),
            stack(lambda o: o[2][0], outs_s), stack(lambda o: o[2][1], outs_s),
            stack(lambda o: o[2][2], outs_s), stack(lambda o: o[3], outs_s))
```

```python
import functools

import jax
import jax.numpy as jnp
from jax import lax
from jax.experimental import pallas as pl
from jax.experimental.pallas import tpu as pltpu

F32 = jnp.float32
BF16 = jnp.bfloat16

HEAD = 64
LORA_W, LORA_A, LORA_G = 64, 64, 160
GMLP_CHUNK = 128
GMLP_GROUPS = 8
WINDOWS = (128, 512, 2048)
DILATIONS = (1, 4, 16)
N_BACK = 128
ATTN_BLOCK = 128
ATTN_UNROLL = 4
NORM_EPS = 1e-6
LN_EPS = 1e-5
GN_EPS = 64e-5
MASK_VALUE = -1e30

LANES = 128
VMEM_LIMIT_BYTES = 48 * 2**20

LORA_PAD = (128, 128, 256)
WKV_CHUNK = 64

NN = ((1,), (0,))
NT = ((1,), (1,))
TN = ((0,), (0,))


def _cparams(*sem):
    return pltpu.CompilerParams(dimension_semantics=sem, vmem_limit_bytes=VMEM_LIMIT_BYTES)


def _dot(a, b, dims=NN):
    return lax.dot_general(a.astype(BF16), b.astype(BF16), (dims, ((), ())),
                           preferred_element_type=F32)


def _rms_matmul_body(x_ref, g_ref, w_ref, o_ref, xn_ref):
    @pl.when(pl.program_id(1) == 0)
    def _():
        x = x_ref[...]
        ms = jnp.mean(x * x, axis=-1, keepdims=True)
        xn_ref[...] = (x * lax.rsqrt(ms + NORM_EPS) * g_ref[...]).astype(BF16)

    o_ref[...] = jnp.dot(xn_ref[...], w_ref[...], preferred_element_type=F32)


def _rms_matmul(x, g, w, tm, tn):
    m, d = x.shape
    n = w.shape[1]
    return pl.pallas_call(
        _rms_matmul_body,
        grid=(m // tm, n // tn),
        in_specs=[pl.BlockSpec((tm, d), lambda i, j: (i, 0)),
                  pl.BlockSpec((1, d), lambda i, j: (0, 0)),
                  pl.BlockSpec((d, tn), lambda i, j: (0, j))],
        out_specs=pl.BlockSpec((tm, tn), lambda i, j: (i, j)),
        out_shape=jax.ShapeDtypeStruct((m, n), F32),
        scratch_shapes=[pltpu.VMEM((tm, d), BF16)],
        compiler_params=_cparams("parallel", "arbitrary"),
        name="rms_matmul",
    )(x, g, w)


def _rwkv_prep_body(z_ref, sp_ref, mu_ref, w0_ref, dw2_ref, a0_ref, a2_ref, g2_ref, kk_ref, ka_ref,
                    r_ref, lw_ref, k2_ref, v_ref, kku_ref, a_ref, g_ref, carry_ref, *, d):
    @pl.when(pl.program_id(1) == 0)
    def _():
        carry_ref[...] = sp_ref[...]

    zr = z_ref[...]
    tt = zr.shape[0]
    row = lax.broadcasted_iota(jnp.int32, zr.shape, 0)
    prev = jnp.where(row == 0, carry_ref[...], pltpu.roll(zr, 1, 0))
    carry_ref[...] = zr[tt - 1:tt, :]
    h = zr + (prev - zr) * mu_ref[...]
    e_w = 3 * d
    e_a = e_w + LORA_PAD[0]
    e_g = e_a + LORA_PAD[1]
    k = h[:, d:2 * d]
    xw = w0_ref[...] + _dot(jnp.tanh(h[:, e_w:e_a]), dw2_ref[...])
    softplus = jnp.maximum(-xw, 0.0) + jnp.log1p(jnp.exp(-jnp.abs(xw)))
    a = jax.nn.sigmoid(a0_ref[...] + _dot(h[:, e_a:e_g], a2_ref[...]))
    r_ref[...] = h[:, :d]
    lw_ref[...] = -jnp.exp(-softplus - 0.5)
    k2_ref[...] = k * (1.0 + (a - 1.0) * ka_ref[...])
    v_ref[...] = h[:, 2 * d:3 * d]
    kku_ref[...] = k * kk_ref[...]
    a_ref[...] = a
    g_ref[...] = _dot(jax.nn.sigmoid(h[:, e_g:]), g2_ref[...])


def _rwkv_prep(z_all, col_block, shift_prev, lp, tt):
    b, t, _ = z_all.shape
    d = lp['k_k'].shape[-1]
    wr = 3 * d + sum(LORA_PAD)
    full = lambda a: pl.BlockSpec(a.shape, lambda i, j: (0,) * a.ndim)
    params = [lp['mu_shift'], lp['decay_w0'], lp['decay_w2'], lp['iclr_a0'], lp['iclr_a2'],
              lp['gate_g2'], lp['k_k'], lp['k_a']]
    out_spec = pl.BlockSpec((None, tt, d), lambda i, j: (i, j, 0))
    return pl.pallas_call(
        functools.partial(_rwkv_prep_body, d=d),
        grid=(b, t // tt),
        in_specs=[pl.BlockSpec((None, tt, wr), lambda i, j: (i, j, col_block)),
                  pl.BlockSpec((None, 1, wr), lambda i, j: (i, 0, 0))] + [full(p) for p in params],
        out_specs=[out_spec] * 7,
        out_shape=[jax.ShapeDtypeStruct((b, t, d), F32)] * 7,
        scratch_shapes=[pltpu.VMEM((1, wr), F32)],
        compiler_params=_cparams("parallel", "arbitrary"),
        name="rwkv_prep",
    )(z_all, shift_prev, *params)


def _each(f, *lists):
    return [f(*args) for args in zip(*lists)]


def _unit_lower_solvers(mats):
    c = mats[0].shape[0]
    bs = min(c, 16)
    nb = c // bs
    assert nb in (1, 2, 4)
    ri = lax.broadcasted_iota(jnp.int32, (c, c), 0)
    ci = lax.broadcasted_iota(jnp.int32, (c, c), 1)
    same = (ri // bs) == (ci // bs)
    eye = jnp.where(ri == ci, 1.0, 0.0)
    pw = [jnp.where(same, a, 0.0) for a in mats]
    tinv = [eye - d for d in pw]
    n = 2
    while n < bs:
        pw = _each(_dot, pw, pw)
        tinv = _each(lambda t, x: t + _dot(t, x), tinv, pw)
        n *= 2
    m = _each(lambda t, a: _dot(t, jnp.where(same, 0.0, a)), tinv, mats) if nb > 1 else None

    def solve(rhs):
        x = _each(_dot, tinv, rhs)
        if nb > 2:
            mx = _each(_dot, m, x)
            x = _each(lambda x_, m_, y_: x_ + _dot(m_, y_), x, m, mx)
        if nb > 1:
            x = _each(lambda x_, m_: x_ - _dot(m_, x_), x, m)
        return x

    return solve


def _cumsum_rows(x):
    c = x.shape[0]
    row = lax.broadcasted_iota(jnp.int32, x.shape, 0)
    s = 1
    while s < c:
        x = x + jnp.where(row >= s, pltpu.roll(x, s, 0), 0.0)
        s *= 2
    return x


def _wkv_chunk_body(r_ref, lw_ref, k2_ref, v_ref, kku_ref, a_ref, g_ref, lnw_ref, lnb_ref, rk_ref,
                    s0_ref, o_ref, s_ref, *, hb):
    @pl.when(pl.program_id(2) == 0)
    def _():
        s_ref[...] = s0_ref[...]

    c = r_ref.shape[0]
    ri = lax.broadcasted_iota(jnp.int32, (c, c), 0)
    ci = lax.broadcasted_iota(jnp.int32, (c, c), 1)
    lower = ri >= ci
    strict = ri > ci
    ek = jnp.where(lax.broadcasted_iota(jnp.int32, (HEAD, HEAD), 0)
                   == lax.broadcasted_iota(jnp.int32, (HEAD, HEAD), 1), 1.0, 0.0)

    lw = lw_ref[...]
    cum = _cumsum_rows(lw)
    gam = jnp.exp(cum)
    gam_prev = jnp.exp(cum - lw)
    igam = jnp.exp(-cum)
    tail = jnp.exp(cum[c - 1:c, :] - cum)

    sls = [slice(HEAD * h, HEAD * (h + 1)) for h in range(hb)]
    cols = lambda x: [x[:, sl] for sl in sls]
    r, k2, v, kku, av = cols(r_ref), cols(k2_ref), cols(v_ref), cols(kku_ref), cols(a_ref)
    kk = [x / jnp.maximum(jnp.sqrt(jnp.sum(x * x, axis=-1, keepdims=True)), 1e-12) for x in kku]
    mul = lambda x, y: x * y
    bvec = _each(mul, kk, av)
    kt = _each(mul, kk, cols(gam_prev))
    rt = _each(mul, r, cols(gam))
    kh = _each(mul, k2, cols(igam))
    bh = _each(mul, bvec, cols(igam))
    kb = _each(mul, k2, cols(tail))
    bb = _each(mul, bvec, cols(tail))

    lhs = _each(lambda x, y: jnp.concatenate([x, y], axis=0), kt, rt)
    sb = _each(lambda x, y: _dot(x, y, NT), lhs, bh)
    sk = _each(lambda x, y: _dot(x, y, NT), lhs, kh)
    a_ub = [jnp.where(strict, x[:c], 0.0) for x in sb]
    a_vk = [jnp.where(strict, x[:c], 0.0) for x in sk]
    b_ub = [jnp.where(lower, x[c:], 0.0) for x in sb]
    b_vk = [jnp.where(lower, x[c:], 0.0) for x in sk]

    solve = _unit_lower_solvers(a_ub)
    av_v = _each(_dot, a_vk, v)
    sol = solve(_each(lambda x, y: jnp.concatenate([x, y], axis=1), kt, av_v))
    bu = _each(_dot, b_ub, sol)
    bv_v = _each(_dot, b_vk, v)
    w_ = [x[:, :HEAD] for x in sol]
    u0 = [x[:, HEAD:] for x in sol]
    q = _each(lambda x, y: x - y[:, :HEAD], rt, bu)
    y0 = _each(lambda x, y: x - y[:, HEAD:], bv_v, bu)
    s_in = [s_ref[h] for h in range(hb)]
    y = _each(lambda q_, s_, y_: _dot(q_, s_, NT) + y_, q, s_in, y0)
    p = _each(lambda w, b_, sl: ek * gam[c - 1:c, sl] - _dot(w, b_, TN), w_, bb, sls)
    z = _each(lambda v_, u_, kb_, bb_: _dot(jnp.concatenate([v_, -u_], axis=0),
                                            jnp.concatenate([kb_, bb_], axis=0), TN), v, u0, kb, bb)
    s_new = _each(lambda s_, p_, z_: _dot(s_, p_) + z_, s_in, p, z)
    for h in range(hb):
        s_ref[h] = s_new[h]

    mu = [jnp.mean(x, axis=-1, keepdims=True) for x in y]
    yc = _each(lambda x, m_: x - m_, y, mu)
    var = [jnp.mean(x * x, axis=-1, keepdims=True) for x in yc]
    for h, sl in enumerate(sls):
        yn = yc[h] * lax.rsqrt(var[h] + GN_EPS) * lnw_ref[:, sl] + lnb_ref[:, sl]
        bonus = jnp.sum(r[h] * k2[h] * rk_ref[:, sl], axis=-1, keepdims=True) * v[h]
        o_ref[:, sl] = (yn + bonus) * g_ref[:, sl]


def _wkv_chunk(parts, lp, s0, chunk, hb):
    b, t, d = parts[0].shape
    nh = d // HEAD
    wb = hb * HEAD
    tok = pl.BlockSpec((None, chunk, wb), lambda i, j, c: (i, c, j))
    par = pl.BlockSpec((1, wb), lambda i, j, c: (0, j))
    st = pl.BlockSpec((None, hb, HEAD, HEAD), lambda i, j, c: (i, j, 0, 0))
    return pl.pallas_call(
        functools.partial(_wkv_chunk_body, hb=hb),
        grid=(b, nh // hb, t // chunk),
        in_specs=[tok] * 7 + [par] * 3 + [st],
        out_specs=[tok, st],
        out_shape=[jax.ShapeDtypeStruct((b, t, d), F32),
                   jax.ShapeDtypeStruct((b, nh, HEAD, HEAD), F32)],
        compiler_params=_cparams("parallel", "parallel", "arbitrary"),
        name="wkv_chunk",
    )(*parts, lp['lnx_w'], lp['lnx_b'], lp['r_k'], s0)


def _gmlp_body(z_ref, lnw_ref, lnb_ref, ws_ref, bs_ref, o_ref, *vn_out, d):
    z = z_ref[...]
    z = 0.5 * z * (1.0 + lax.erf(z * (2.0 ** -0.5)))
    u, vv = z[:, :d], z[:, d:]
    mu = jnp.mean(vv, axis=-1, keepdims=True)
    vc = vv - mu
    var = jnp.mean(vc * vc, axis=-1, keepdims=True)
    vn = vc * lax.rsqrt(var + LN_EPS) * lnw_ref[...] + lnb_ref[...]
    if vn_out:
        vn_out[0][...] = vn
    c = z.shape[0]
    causal = (lax.broadcasted_iota(jnp.int32, (c, c), 0) >= lax.broadcasted_iota(jnp.int32, (c, c), 1))
    cg = d // GMLP_GROUPS
    for g in range(GMLP_GROUPS):
        sl = slice(cg * g, cg * (g + 1))
        mixed = _dot(jnp.where(causal, ws_ref[g], 0.0), vn[:, sl])
        o_ref[:, sl] = u[:, sl] * (mixed + bs_ref[:, sl])


def _gmlp(z_all, col_block, lp, chunk, want_vn):
    b, t, _ = z_all.shape
    d = lp['lnv_w'].shape[-1]
    ws, bs = lp['w_s'][:, :chunk, :chunk], lp['b_s_cols'][:chunk]
    full = lambda a: pl.BlockSpec(a.shape, lambda i, j: (0,) * a.ndim)
    out_spec = pl.BlockSpec((None, chunk, d), lambda i, j: (i, j, 0))
    n_out = 2 if want_vn else 1
    return pl.pallas_call(
        functools.partial(_gmlp_body, d=d),
        grid=(b, t // chunk),
        in_specs=[pl.BlockSpec((None, chunk, 2 * d), lambda i, j: (i, j, col_block)),
                  full(lp['lnv_w']), full(lp['lnv_b']), full(ws), full(bs)],
        out_specs=[out_spec] * n_out,
        out_shape=[jax.ShapeDtypeStruct((b, t, d), F32)] * n_out,
        compiler_params=_cparams("parallel", "parallel"),
        name="gmlp",
    )(z_all, lp['lnv_w'], lp['lnv_b'], ws, bs)


def _attn_group(slope_ref, q_ref, k_ref, v_ref, o_ref, ls_ref, m_ref, dil):
    t = q_ref.shape[0]
    qb = ATTN_BLOCK
    nbk = t // (dil * qb)

    def own_lanes(hh, nrows):
        lane = lax.broadcasted_iota(jnp.int32, (nrows, LANES), 1)
        return lane < HEAD if hh == 0 else lane >= HEAD

    first = own_lanes(0, qb)
    mine = (first, own_lanes(1, qb))

    def rows(start, size):
        return pl.ds(start, size) if dil == 1 else pl.ds(start, size, stride=dil)

    def make_bias(span, shift):
        rel = (shift + lax.broadcasted_iota(jnp.int32, (qb, span), 0)
               - lax.broadcasted_iota(jnp.int32, (qb, span), 1))
        valid = (rel >= 0) & (rel <= N_BACK)
        dist = (rel * dil).astype(F32)
        return [jnp.where(valid, -slope_ref[:, HEAD * hh:HEAD * hh + 1] * dist, MASK_VALUE)
                for hh in range(2)]

    def update(blocks, bias):
        q = [q_ref[qr, :] * (HEAD ** -0.5) for qr, _ in blocks]
        k = [k_ref[kr, :].astype(BF16) for _, kr in blocks]
        v = [v_ref[kr, :] for _, kr in blocks]
        units = [(u, hh) for u in range(len(blocks)) for hh in range(2)]
        s = [_dot(jnp.where(mine[hh], q[u], 0.0), k[u], NT) + bias[hh] for u, hh in units]
        mp = [m_ref[hh, blocks[u][0], :] for u, hh in units]
        rm = [jnp.broadcast_to(jnp.max(x, axis=-1, keepdims=True), (qb, LANES)) for x in s]
        mn = _each(jnp.maximum, mp, rm)
        alpha = _each(lambda a, b: jnp.exp(a - b), mp, mn)
        p = _each(lambda x, m_: jnp.exp(x - jnp.concatenate([m_] * (x.shape[1] // LANES), axis=1)), s, mn)
        vo = [jnp.where(own_lanes(hh, v[u].shape[0]), v[u], 1.0).astype(BF16) for u, hh in units]
        pv = _each(_dot, p, vo)
        done = []
        for u, (qr, _) in enumerate(blocks):
            a0, a1, r0, r1 = alpha[2 * u], alpha[2 * u + 1], pv[2 * u], pv[2 * u + 1]
            acc = o_ref[qr, :] * jnp.where(first, a0, a1) + jnp.where(first, r0, r1)
            den = ls_ref[qr, :] * jnp.where(first, a1, a0) + jnp.where(first, r1, r0)
            done.append((qr, acc, den, mn[2 * u], mn[2 * u + 1]))
        for qr, acc, den, m0, m1 in done:
            o_ref[qr, :] = acc
            ls_ref[qr, :] = den
            m_ref[0, qr, :] = m0
            m_ref[1, qr, :] = m1

    def run(nblocks, rows_fn, bias):
        main = nblocks // ATTN_UNROLL

        def body(width, base):
            def trip(i, carry):
                update([rows_fn(base + i * width + u) for u in range(width)], bias)
                return carry
            return trip

        if main:
            lax.fori_loop(0, main, body(ATTN_UNROLL, 0), 0)
        if nblocks - main * ATTN_UNROLL:
            lax.fori_loop(0, nblocks - main * ATTN_UNROLL, body(1, main * ATTN_UNROLL), 0)

    run(dil, lambda i: (rows(i, qb), rows(i, qb)), make_bias(qb, 0))

    def later(i):
        res = i // (nbk - 1)
        n = 1 + i % (nbk - 1)
        return rows(res + n * (qb * dil), qb), rows(res + (n - 1) * (qb * dil), 2 * qb)

    run(dil * (nbk - 1), later, make_bias(2 * qb, qb))


def _attn_prompt_body(slope_ref, q_ref, k_ref, v_ref, o_ref, ls_ref, m_ref):
    g = pl.program_id(2)

    @pl.when(g == 0)
    def _():
        o_ref[...] = jnp.zeros_like(o_ref)
        ls_ref[...] = jnp.zeros_like(ls_ref)
        m_ref[...] = jnp.full_like(m_ref, MASK_VALUE)

    for gi, dil in enumerate(DILATIONS):
        @pl.when(g == gi)
        def _(dil=dil):
            _attn_group(slope_ref, q_ref, k_ref, v_ref, o_ref, ls_ref, m_ref, dil)

    @pl.when(g == len(DILATIONS) - 1)
    def _():
        o_ref[...] = o_ref[...] / pltpu.roll(ls_ref[...], HEAD, 1)


def _attn_prompt(z_all, col_off, d, slopes):
    b, t, _ = z_all.shape
    assert t % (max(DILATIONS) * 2 * ATTN_BLOCK) == 0
    npair = d // LANES
    base = col_off // LANES
    spec = lambda j: pl.BlockSpec((None, t, LANES),
                                  lambda i, p, g: (i, 0, base + (3 * g + j) * npair + p))
    return pl.pallas_call(
        _attn_prompt_body,
        grid=(b, npair, len(DILATIONS)),
        in_specs=[pl.BlockSpec((None, 1, LANES), lambda i, p, g: (p, 0, 0)), spec(0), spec(1), spec(2)],
        out_specs=pl.BlockSpec((None, t, LANES), lambda i, p, g: (i, 0, p)),
        out_shape=jax.ShapeDtypeStruct((b, t, d), F32),
        scratch_shapes=[pltpu.VMEM((t, LANES), F32), pltpu.VMEM((2, t, LANES), F32)],
        compiler_params=_cparams("parallel", "parallel", "arbitrary"),
        name="attn_prompt",
    )(slopes, z_all, z_all, z_all)


def _attn_sample_body(slope_ref, *refs):
    ng = len(DILATIONS)
    new_refs, cache_refs, o_ref = refs[:3 * ng], refs[3 * ng:4 * ng], refs[4 * ng]
    s_len = o_ref.shape[0]
    rows = 2 * s_len
    row = lax.broadcasted_iota(jnp.int32, (rows, LANES), 0)
    lane = lax.broadcasted_iota(jnp.int32, (rows, LANES), 1)
    own = (lane < HEAD) == (row < s_len)
    slope = jnp.where(row[:, :1] < s_len, slope_ref[:, 0:1], slope_ref[:, HEAD:HEAD + 1])
    pad = jnp.zeros((s_len, LANES), F32)
    parts = []
    for gi, dil in enumerate(DILATIONS):
        assert dil & (dil - 1) == 0
        q_ref, kn_ref, vn_ref = new_refs[3 * gi:3 * gi + 3]
        q = q_ref[...] * (HEAD ** -0.5)
        q2 = jnp.where(own, jnp.concatenate([q, q], axis=0), 0.0)
        kt = cache_refs[gi][0]
        vt = cache_refs[gi][1]
        win = kt.shape[1]
        qpos = lax.broadcasted_iota(jnp.int32, (rows, win), 0) % s_len
        gap = qpos + win - lax.broadcasted_iota(jnp.int32, (rows, win), 1)
        ok = (gap <= win) & ((gap & (dil - 1)) == 0)
        sc = jnp.where(ok, _dot(q2, kt) - slope * gap.astype(F32), MASK_VALUE)
        kn = jnp.concatenate([kn_ref[...], pad], axis=0)
        vn = jnp.concatenate([vn_ref[...], pad], axis=0)
        gapn = (lax.broadcasted_iota(jnp.int32, (rows, rows), 0) % s_len
                - lax.broadcasted_iota(jnp.int32, (rows, rows), 1))
        okn = (gapn >= 0) & ((gapn & (dil - 1)) == 0)
        sn = jnp.where(okn, _dot(q2, kn, NT) - slope * gapn.astype(F32), MASK_VALUE)
        m = jnp.maximum(jnp.max(sc, axis=-1, keepdims=True), jnp.max(sn, axis=-1, keepdims=True))
        pc = jnp.exp(sc - m)
        pn = jnp.exp(sn - m)
        den = jnp.sum(pc, axis=-1, keepdims=True) + jnp.sum(pn, axis=-1, keepdims=True)
        num = _dot(pc, vt, NT) + _dot(pn, vn)
        parts.append((m, den, num))
    m_all = functools.reduce(jnp.maximum, [p[0] for p in parts])
    den = sum(p[1] * jnp.exp(p[0] - m_all) for p in parts)
    num = sum(p[2] * jnp.exp(p[0] - m_all) for p in parts)
    o2 = num / den
    o_ref[...] = jnp.where(lane[:s_len] < HEAD, o2[:s_len], o2[s_len:])


def _attn_sample(z_all, col_off, d, caches_t, slopes):
    b, s_len, _ = z_all.shape
    npair = d // LANES
    base = col_off // LANES
    new_specs = [pl.BlockSpec((None, s_len, LANES),
                              lambda i, p, g=g, j=j: (i, 0, base + (3 * g + j) * npair + p))
                 for g in range(len(DILATIONS)) for j in range(3)]
    cache_specs = [pl.BlockSpec((None, 2, LANES, c.shape[-1]), lambda i, p: (i, 0, p, 0)) for c in caches_t]
    return pl.pallas_call(
        _attn_sample_body,
        grid=(b, npair),
        in_specs=[pl.BlockSpec((None, 1, LANES), lambda i, p: (p, 0, 0))] + new_specs + cache_specs,
        out_specs=pl.BlockSpec((None, s_len, LANES), lambda i, p: (i, 0, p)),
        out_shape=jax.ShapeDtypeStruct((b, s_len, d), F32),
        compiler_params=_cparams("parallel", "parallel"),
        name="attn_sample",
    )(slopes, *([z_all] * len(new_specs)), *caches_t)


def _merge_body(x_ref, oa_ref, ob_ref, oc_ref, zg_ref, bg_ref, wb_ref, wo_ref, o_ref, *, d):
    acc = None
    for n, br in enumerate((oa_ref, ob_ref, oc_ref)):
        sl = slice(n * d, (n + 1) * d)
        term = _dot(br[...], wb_ref[n]) * jax.nn.sigmoid(zg_ref[:, sl] + bg_ref[:, sl])
        acc = term if acc is None else acc + term
    o_ref[...] = x_ref[...] + _dot(acc, wo_ref[...])


def _merge(x, oa, ob, oc, z_all, gate_block, lp, tm):
    m, d = x.shape
    row = pl.BlockSpec((tm, d), lambda i: (i, 0))
    full = lambda a: pl.BlockSpec(a.shape, lambda i: (0,) * a.ndim)
    return pl.pallas_call(
        functools.partial(_merge_body, d=d),
        grid=(m // tm,),
        in_specs=[row, row, row, row, pl.BlockSpec((tm, 3 * d), lambda i: (i, gate_block)),
                  full(lp['b_gate']), full(lp['w_branch']), full(lp['w_out'])],
        out_specs=row,
        out_shape=jax.ShapeDtypeStruct((m, d), F32),
        compiler_params=_cparams("parallel"),
        name="merge",
    )(x, oa, ob, oc, z_all, lp['b_gate'], lp['w_branch'], lp['w_out'])


def _mlp_body(x_ref, g_ref, wu_ref, wd_ref, gf_ref, o_ref, xn_ref, acc_ref, *, final_norm):
    j = pl.program_id(1)

    @pl.when(j == 0)
    def _():
        x = x_ref[...]
        ms = jnp.mean(x * x, axis=-1, keepdims=True)
        xn_ref[...] = (x * lax.rsqrt(ms + NORM_EPS) * g_ref[...]).astype(BF16)
        acc_ref[...] = jnp.zeros_like(acc_ref)

    h = jnp.maximum(jnp.dot(xn_ref[...], wu_ref[...], preferred_element_type=F32), 0.0)
    acc_ref[...] += _dot(h * h, wd_ref[...])

    @pl.when(j == pl.num_programs(1) - 1)
    def _():
        y = x_ref[...] + acc_ref[...]
        if final_norm:
            ms = jnp.mean(y * y, axis=-1, keepdims=True)
            y = y * lax.rsqrt(ms + NORM_EPS) * gf_ref[...]
        o_ref[...] = y


def _mlp(x, lp, g_final, final_norm, tm, tf):
    m, d = x.shape
    ff = lp['w_up'].shape[1]
    vec = pl.BlockSpec((1, d), lambda i, j: (0, 0))
    row = pl.BlockSpec((tm, d), lambda i, j: (i, 0))
    return pl.pallas_call(
        functools.partial(_mlp_body, final_norm=final_norm),
        grid=(m // tm, ff // tf),
        in_specs=[row, vec, pl.BlockSpec((d, tf), lambda i, j: (0, j)),
                  pl.BlockSpec((tf, d), lambda i, j: (j, 0)), vec],
        out_specs=row,
        out_shape=jax.ShapeDtypeStruct((m, d), F32),
        scratch_shapes=[pltpu.VMEM((tm, d), BF16), pltpu.VMEM((tm, d), F32)],
        compiler_params=_cparams("parallel", "arbitrary"),
        name="mlp",
    )(x, lp['ln_mlp'], lp['w_up'], lp['w_down'], g_final)


def _pad_cols(a, width):
    return jnp.pad(a, [(0, 0)] * (a.ndim - 1) + [(0, width - a.shape[-1])])


def _pad_rows(a, height):
    return jnp.pad(a, [(0, height - a.shape[0]), (0, 0)])


def _rwkv_cols(a, d):
    e = 3 * d
    pieces = [a[..., :e], _pad_cols(a[..., e:e + LORA_W], LORA_PAD[0]),
              _pad_cols(a[..., e + LORA_W:e + LORA_W + LORA_A], LORA_PAD[1]),
              _pad_cols(a[..., e + LORA_W + LORA_A:], LORA_PAD[2])]
    return jnp.concatenate(pieces, axis=-1)


def _rwkv_cols_inverse(a, d):
    e = 3 * d
    o1 = e + LORA_PAD[0]
    o2 = o1 + LORA_PAD[1]
    return jnp.concatenate([a[..., :e], a[..., e:e + LORA_W], a[..., o1:o1 + LORA_A],
                            a[..., o2:o2 + LORA_G]], axis=-1)


def _layer_params(l, d, ln_mix, w_in, mu_shift, decay_w0, decay_w2, iclr_a0, iclr_a2, gate_g2, k_k, k_a,
                  r_k, lnx_w, lnx_b, lnv_w, lnv_b, w_s, b_s, b_gate, w_branch, w_out, ln_mlp, w_up, w_down):
    shift_w = 3 * d + LORA_W + LORA_A + LORA_G
    e1 = shift_w
    e2 = e1 + 2 * d
    e3 = e2 + 9 * d
    w = w_in[l]
    w_perm = jnp.concatenate([w[:, e2:e3], w[:, e3:], w[:, e1:e2], _rwkv_cols(w[:, :e1], d)], axis=1)
    row = lambda a: a[l].reshape(1, -1)
    return dict(
        ln_mix=row(ln_mix), w_in=w_perm.astype(BF16), mu_shift=_rwkv_cols(row(mu_shift), d),
        decay_w0=row(decay_w0), decay_w2=_pad_rows(decay_w2[l], LORA_PAD[0]).astype(BF16),
        iclr_a0=row(iclr_a0), iclr_a2=_pad_rows(iclr_a2[l], LORA_PAD[1]).astype(BF16),
        gate_g2=_pad_rows(gate_g2[l], LORA_PAD[2]).astype(BF16),
        k_k=row(k_k), k_a=row(k_a), r_k=row(r_k), lnx_w=row(lnx_w), lnx_b=row(lnx_b),
        lnv_w=row(lnv_w), lnv_b=row(lnv_b), w_s=w_s[l],
        b_s_cols=jnp.repeat(b_s[l].T, d // GMLP_GROUPS, axis=1),
        b_gate=row(b_gate), w_branch=w_branch[l].astype(BF16), w_out=w_out[l].astype(BF16),
        ln_mlp=row(ln_mlp), w_up=w_up[l].astype(BF16), w_down=w_down[l].astype(BF16))


def _tile(n, pref):
    while n % pref:
        pref //= 2
    return pref


def _layer(x, lp, g_final, final_norm, shift_prev, wkv_prev, kv_prev, slopes_pair):
    b, t, d = x.shape
    m = b * t
    wr = 3 * d + sum(LORA_PAD)
    off_gate, off_gmlp, off_rwkv = 9 * d, 12 * d, 14 * d
    x2 = x.reshape(m, d)
    z_all = _rms_matmul(x2, lp['ln_mix'], lp['w_in'], _tile(m, 1024), 512).reshape(b, t, -1)

    sp = _rwkv_cols(shift_prev, d)[:, None, :]
    parts = _rwkv_prep(z_all, off_rwkv // wr, sp, lp, _tile(t, 128))
    o_a, wkv = _wkv_chunk(parts, lp, wkv_prev, _tile(t, WKV_CHUNK), 8)
    shift_last = _rwkv_cols_inverse(z_all[:, t - 1, off_rwkv:], d)

    nh = d // HEAD
    if kv_prev is None:
        (o_b,), v_rows = _gmlp(z_all, off_gmlp // (2 * d), lp, GMLP_CHUNK, False), None
        o_c = _attn_prompt(z_all, 0, d, slopes_pair)
    else:
        o_b, v_rows = _gmlp(z_all, off_gmlp // (2 * d), lp, t, True)
        caches_t = [jnp.transpose(c, (0, 1, 3, 4, 2)).reshape(b, 2, d, c.shape[2]) for c in kv_prev]
        o_c = _attn_sample(z_all, 0, d, caches_t, slopes_pair)
    kv_rows = []
    for gi, win in enumerate(WINDOWS):
        keep = min(win, t)
        kv = z_all[:, t - keep:, (3 * gi + 1) * d:(3 * gi + 3) * d].reshape(b, keep, 2, nh, HEAD)
        kv_rows.append(jnp.swapaxes(kv, 1, 2))

    hmid = _merge(x2, o_a.reshape(m, d), o_b.reshape(m, d), o_c.reshape(m, d),
                  z_all.reshape(m, -1), off_gate // (3 * d), lp, _tile(m, 256))
    out = _mlp(hmid, lp, g_final, final_norm, _tile(m, 512), 1024)
    return out.reshape(b, t, d), shift_last, wkv, kv_rows, v_rows


def kernel(x_prompt, x_sample, state_wkv, state_shift, cache_kv_w128, cache_kv_w512, cache_kv_w2048,
           ln_mix, w_in, mu_shift, decay_w0, decay_w2, iclr_a0, iclr_a2, gate_g2, k_k, k_a, r_k,
           lnx_w, lnx_b, lnv_w, lnv_b, w_s, b_s, b_gate, w_branch, w_out, ln_mlp, w_up, w_down,
           ln_final):
    depth = w_in.shape[0]
    d = x_prompt.shape[-1]
    nh = d // HEAD
    nb_prompt = x_prompt.shape[0]
    shift_w = state_shift.shape[-1]
    slopes = jnp.exp2(-8.0 * jnp.arange(1, nh + 1, dtype=F32) / nh)
    slopes_pair = jnp.repeat(slopes, HEAD).reshape(nh // 2, 1, LANES)
    g_final = ln_final.reshape(1, d)
    caches = (cache_kv_w128, cache_kv_w512, cache_kv_w2048)

    xp, xs = x_prompt, x_sample
    outs_p, outs_s = [], []
    for l in range(depth):
        lp = _layer_params(l, d, ln_mix, w_in, mu_shift, decay_w0, decay_w2, iclr_a0, iclr_a2, gate_g2,
                           k_k, k_a, r_k, lnx_w, lnx_b, lnv_w, lnv_b, w_s, b_s, b_gate, w_branch, w_out,
                           ln_mlp, w_up, w_down)
        last = l == depth - 1
        xp, sh, wk, kv, _ = _layer(xp, lp, g_final, last, jnp.zeros((nb_prompt, shift_w), F32),
                                   jnp.zeros((nb_prompt, nh, HEAD, HEAD), F32), None, slopes_pair)
        outs_p.append((wk, sh, kv))
        xs, sh, wk, kv, vr = _layer(xs, lp, g_final, last, state_shift[l], state_wkv[l],
                                    tuple(c[l] for c in caches), slopes_pair)
        outs_s.append((wk, sh, kv, vr))

    stack = lambda f, outs: jnp.stack([f(o) for o in outs])
    s_len = x_sample.shape[1]
    s_kv = [jnp.concatenate([c[:, :, :, s_len:], stack(lambda o: o[2][gi], outs_s)], axis=3)
            for gi, c in enumerate(caches)]
    return (xp, xs,
            stack(lambda o: o[0], outs_p), stack(lambda o: o[1], outs_p),
            stack(lambda o: o[2][0], outs_p), stack(lambda o: o[2][1], outs_p),
            stack(lambda o: o[2][2], outs_p),
            stack(lambda o: o[0], outs_s), stack(lambda o: o[1], outs_s),
            s_kv[0], s_kv[1], s_kv[2], stack(lambda o: o[3], outs_s))
```

```python
import functools

import jax
import jax.numpy as jnp
from jax import lax
from jax.experimental import pallas as pl
from jax.experimental.pallas import tpu as pltpu

F32 = jnp.float32
BF16 = jnp.bfloat16

HEAD = 64
LORA_W, LORA_A, LORA_G = 64, 64, 160
GMLP_CHUNK = 128
GMLP_GROUPS = 8
WINDOWS = (128, 512, 2048)
DILATIONS = (1, 4, 16)
N_BACK = 128
ATTN_BLOCK = 128
ATTN_UNROLL = 4
NORM_EPS = 1e-6
LN_EPS = 1e-5
GN_EPS = 64e-5
MASK_VALUE = -1e30

LANES = 128
VMEM_LIMIT_BYTES = 48 * 2**20

LORA_PAD = (128, 128, 256)
WKV_CHUNK = 64

NN = ((1,), (0,))
NT = ((1,), (1,))
TN = ((0,), (0,))


def _cparams(*sem):
    return pltpu.CompilerParams(dimension_semantics=sem, vmem_limit_bytes=VMEM_LIMIT_BYTES)


def _dot(a, b, dims=NN):
    return lax.dot_general(a.astype(BF16), b.astype(BF16), (dims, ((), ())),
                           preferred_element_type=F32)


def _rms_matmul_body(x_ref, g_ref, w_ref, o_ref, xn_ref):
    @pl.when(pl.program_id(1) == 0)
    def _():
        x = x_ref[...]
        ms = jnp.mean(x * x, axis=-1, keepdims=True)
        xn_ref[...] = (x * lax.rsqrt(ms + NORM_EPS) * g_ref[...]).astype(BF16)

    o_ref[...] = jnp.dot(xn_ref[...], w_ref[...], preferred_element_type=F32)


def _rms_matmul(x, g, w, tm, tn):
    m, d = x.shape
    n = w.shape[1]
    return pl.pallas_call(
        _rms_matmul_body,
        grid=(m // tm, n // tn),
        in_specs=[pl.BlockSpec((tm, d), lambda i, j: (i, 0)),
                  pl.BlockSpec((1, d), lambda i, j: (0, 0)),
                  pl.BlockSpec((d, tn), lambda i, j: (0, j))],
        out_specs=pl.BlockSpec((tm, tn), lambda i, j: (i, j)),
        out_shape=jax.ShapeDtypeStruct((m, n), F32),
        scratch_shapes=[pltpu.VMEM((tm, d), BF16)],
        compiler_params=_cparams("parallel", "arbitrary"),
        name="rms_matmul",
    )(x, g, w)


def _rwkv_prep_body(z_ref, sp_ref, mu_ref, w0_ref, dw2_ref, a0_ref, a2_ref, g2_ref, kk_ref, ka_ref,
                    r_ref, lw_ref, k2_ref, v_ref, kku_ref, a_ref, g_ref, carry_ref, *, d):
    @pl.when(pl.program_id(1) == 0)
    def _():
        carry_ref[...] = sp_ref[...]

    zr = z_ref[...]
    tt = zr.shape[0]
    row = lax.broadcasted_iota(jnp.int32, zr.shape, 0)
    prev = jnp.where(row == 0, carry_ref[...], pltpu.roll(zr, 1, 0))
    carry_ref[...] = zr[tt - 1:tt, :]
    h = zr + (prev - zr) * mu_ref[...]
    e_w = 3 * d
    e_a = e_w + LORA_PAD[0]
    e_g = e_a + LORA_PAD[1]
    k = h[:, d:2 * d]
    xw = w0_ref[...] + _dot(jnp.tanh(h[:, e_w:e_a]), dw2_ref[...])
    softplus = jnp.maximum(-xw, 0.0) + jnp.log1p(jnp.exp(-jnp.abs(xw)))
    a = jax.nn.sigmoid(a0_ref[...] + _dot(h[:, e_a:e_g], a2_ref[...]))
    r_ref[...] = h[:, :d]
    lw_ref[...] = -jnp.exp(-softplus - 0.5)
    k2_ref[...] = k * (1.0 + (a - 1.0) * ka_ref[...])
    v_ref[...] = h[:, 2 * d:3 * d]
    kku_ref[...] = k * kk_ref[...]
    a_ref[...] = a
    g_ref[...] = _dot(jax.nn.sigmoid(h[:, e_g:]), g2_ref[...])


def _rwkv_prep(z_all, col_block, shift_prev, lp, tt):
    b, t, _ = z_all.shape
    d = lp['k_k'].shape[-1]
    wr = 3 * d + sum(LORA_PAD)
    full = lambda a: pl.BlockSpec(a.shape, lambda i, j: (0,) * a.ndim)
    params = [lp['mu_shift'], lp['decay_w0'], lp['decay_w2'], lp['iclr_a0'], lp['iclr_a2'],
              lp['gate_g2'], lp['k_k'], lp['k_a']]
    out_spec = pl.BlockSpec((None, tt, d), lambda i, j: (i, j, 0))
    return pl.pallas_call(
        functools.partial(_rwkv_prep_body, d=d),
        grid=(b, t // tt),
        in_specs=[pl.BlockSpec((None, tt, wr), lambda i, j: (i, j, col_block)),
                  pl.BlockSpec((None, 1, wr), lambda i, j: (i, 0, 0))] + [full(p) for p in params],
        out_specs=[out_spec] * 7,
        out_shape=[jax.ShapeDtypeStruct((b, t, d), F32)] * 7,
        scratch_shapes=[pltpu.VMEM((1, wr), F32)],
        compiler_params=_cparams("parallel", "arbitrary"),
        name="rwkv_prep",
    )(z_all, shift_prev, *params)


def _each(f, *lists):
    return [f(*args) for args in zip(*lists)]


def _unit_lower_solvers(mats):
    c = mats[0].shape[0]
    bs = min(c, 16)
    nb = c // bs
    assert nb in (1, 2, 4)
    ri = lax.broadcasted_iota(jnp.int32, (c, c), 0)
    ci = lax.broadcasted_iota(jnp.int32, (c, c), 1)
    same = (ri // bs) == (ci // bs)
    eye = jnp.where(ri == ci, 1.0, 0.0)
    pw = [jnp.where(same, a, 0.0) for a in mats]
    tinv = [eye - d for d in pw]
    n = 2
    while n < bs:
        pw = _each(_dot, pw, pw)
        tinv = _each(lambda t, x: t + _dot(t, x), tinv, pw)
        n *= 2
    m = _each(lambda t, a: _dot(t, jnp.where(same, 0.0, a)), tinv, mats) if nb > 1 else None

    def solve(rhs):
        x = _each(_dot, tinv, rhs)
        if nb > 2:
            mx = _each(_dot, m, x)
            x = _each(lambda x_, m_, y_: x_ + _dot(m_, y_), x, m, mx)
        if nb > 1:
            x = _each(lambda x_, m_: x_ - _dot(m_, x_), x, m)
        return x

    return solve


def _cumsum_rows(x):
    c = x.shape[0]
    row = lax.broadcasted_iota(jnp.int32, x.shape, 0)
    s = 1
    while s < c:
        x = x + jnp.where(row >= s, pltpu.roll(x, s, 0), 0.0)
        s *= 2
    return x


def _wkv_chunk_body(r_ref, lw_ref, k2_ref, v_ref, kku_ref, a_ref, g_ref, lnw_ref, lnb_ref, rk_ref,
                    s0_ref, o_ref, s_ref, *, hb):
    @pl.when(pl.program_id(2) == 0)
    def _():
        s_ref[...] = s0_ref[...]

    c = r_ref.shape[0]
    ri = lax.broadcasted_iota(jnp.int32, (c, c), 0)
    ci = lax.broadcasted_iota(jnp.int32, (c, c), 1)
    lower = ri >= ci
    strict = ri > ci
    ek = jnp.where(lax.broadcasted_iota(jnp.int32, (HEAD, HEAD), 0)
                   == lax.broadcasted_iota(jnp.int32, (HEAD, HEAD), 1), 1.0, 0.0)

    lw = lw_ref[...]
    cum = _cumsum_rows(lw)
    gam = jnp.exp(cum)
    gam_prev = jnp.exp(cum - lw)
    igam = jnp.exp(-cum)
    tail = jnp.exp(cum[c - 1:c, :] - cum)

    sls = [slice(HEAD * h, HEAD * (h + 1)) for h in range(hb)]
    cols = lambda x: [x[:, sl] for sl in sls]
    r, k2, v, kku, av = cols(r_ref), cols(k2_ref), cols(v_ref), cols(kku_ref), cols(a_ref)
    kk = [x / jnp.maximum(jnp.sqrt(jnp.sum(x * x, axis=-1, keepdims=True)), 1e-12) for x in kku]
    mul = lambda x, y: x * y
    bvec = _each(mul, kk, av)
    kt = _each(mul, kk, cols(gam_prev))
    rt = _each(mul, r, cols(gam))
    kh = _each(mul, k2, cols(igam))
    bh = _each(mul, bvec, cols(igam))
    kb = _each(mul, k2, cols(tail))
    bb = _each(mul, bvec, cols(tail))

    lhs = _each(lambda x, y: jnp.concatenate([x, y], axis=0), kt, rt)
    sb = _each(lambda x, y: _dot(x, y, NT), lhs, bh)
    sk = _each(lambda x, y: _dot(x, y, NT), lhs, kh)
    a_ub = [jnp.where(strict, x[:c], 0.0) for x in sb]
    a_vk = [jnp.where(strict, x[:c], 0.0) for x in sk]
    b_ub = [jnp.where(lower, x[c:], 0.0) for x in sb]
    b_vk = [jnp.where(lower, x[c:], 0.0) for x in sk]

    solve = _unit_lower_solvers(a_ub)
    av_v = _each(_dot, a_vk, v)
    sol = solve(_each(lambda x, y: jnp.concatenate([x, y], axis=1), kt, av_v))
    bu = _each(_dot, b_ub, sol)
    bv_v = _each(_dot, b_vk, v)
    w_ = [x[:, :HEAD] for x in sol]
    u0 = [x[:, HEAD:] for x in sol]
    q = _each(lambda x, y: x - y[:, :HEAD], rt, bu)
    y0 = _each(lambda x, y: x - y[:, HEAD:], bv_v, bu)
    s_in = [s_ref[h] for h in range(hb)]
    y = _each(lambda q_, s_, y_: _dot(q_, s_, NT) + y_, q, s_in, y0)
    p = _each(lambda w, b_, sl: ek * gam[c - 1:c, sl] - _dot(w, b_, TN), w_, bb, sls)
    z = _each(lambda v_, u_, kb_, bb_: _dot(jnp.concatenate([v_, -u_], axis=0),
                                            jnp.concatenate([kb_, bb_], axis=0), TN), v, u0, kb, bb)
    s_new = _each(lambda s_, p_, z_: _dot(s_, p_) + z_, s_in, p, z)
    for h in range(hb):
        s_ref[h] = s_new[h]

    mu = [jnp.mean(x, axis=-1, keepdims=True) for x in y]
    yc = _each(lambda x, m_: x - m_, y, mu)
    var = [jnp.mean(x * x, axis=-1, keepdims=True) for x in yc]
    for h, sl in enumerate(sls):
        yn = yc[h] * lax.rsqrt(var[h] + GN_EPS) * lnw_ref[:, sl] + lnb_ref[:, sl]
        bonus = jnp.sum(r[h] * k2[h] * rk_ref[:, sl], axis=-1, keepdims=True) * v[h]
        o_ref[:, sl] = (yn + bonus) * g_ref[:, sl]


def _wkv_chunk(parts, lp, s0, chunk, hb):
    b, t, d = parts[0].shape
    nh = d // HEAD
    wb = hb * HEAD
    tok = pl.BlockSpec((None, chunk, wb), lambda i, j, c: (i, c, j))
    par = pl.BlockSpec((1, wb), lambda i, j, c: (0, j))
    st = pl.BlockSpec((None, hb, HEAD, HEAD), lambda i, j, c: (i, j, 0, 0))
    return pl.pallas_call(
        functools.partial(_wkv_chunk_body, hb=hb),
        grid=(b, nh // hb, t // chunk),
        in_specs=[tok] * 7 + [par] * 3 + [st],
        out_specs=[tok, st],
        out_shape=[jax.ShapeDtypeStruct((b, t, d), F32),
                   jax.ShapeDtypeStruct((b, nh, HEAD, HEAD), F32)],
        compiler_params=_cparams("parallel", "parallel", "arbitrary"),
        name="wkv_chunk",
    )(*parts, lp['lnx_w'], lp['lnx_b'], lp['r_k'], s0)


def _gmlp_body(z_ref, lnw_ref, lnb_ref, ws_ref, bs_ref, o_ref, *vn_out, d):
    z = z_ref[...]
    z = 0.5 * z * (1.0 + lax.erf(z * (2.0 ** -0.5)))
    u, vv = z[:, :d], z[:, d:]
    mu = jnp.mean(vv, axis=-1, keepdims=True)
    vc = vv - mu
    var = jnp.mean(vc * vc, axis=-1, keepdims=True)
    vn = vc * lax.rsqrt(var + LN_EPS) * lnw_ref[...] + lnb_ref[...]
    if vn_out:
        vn_out[0][...] = vn
    c = z.shape[0]
    causal = (lax.broadcasted_iota(jnp.int32, (c, c), 0) >= lax.broadcasted_iota(jnp.int32, (c, c), 1))
    cg = d // GMLP_GROUPS
    for g in range(GMLP_GROUPS):
        sl = slice(cg * g, cg * (g + 1))
        mixed = _dot(jnp.where(causal, ws_ref[g], 0.0), vn[:, sl])
        o_ref[:, sl] = u[:, sl] * (mixed + bs_ref[:, sl])


def _gmlp(z_all, col_block, lp, chunk, want_vn):
    b, t, _ = z_all.shape
    d = lp['lnv_w'].shape[-1]
    ws, bs = lp['w_s'][:, :chunk, :chunk], lp['b_s_cols'][:chunk]
    full = lambda a: pl.BlockSpec(a.shape, lambda i, j: (0,) * a.ndim)
    out_spec = pl.BlockSpec((None, chunk, d), lambda i, j: (i, j, 0))
    n_out = 2 if want_vn else 1
    return pl.pallas_call(
        functools.partial(_gmlp_body, d=d),
        grid=(b, t // chunk),
        in_specs=[pl.BlockSpec((None, chunk, 2 * d), lambda i, j: (i, j, col_block)),
                  full(lp['lnv_w']), full(lp['lnv_b']), full(ws), full(bs)],
        out_specs=[out_spec] * n_out,
        out_shape=[jax.ShapeDtypeStruct((b, t, d), F32)] * n_out,
        compiler_params=_cparams("parallel", "parallel"),
        name="gmlp",
    )(z_all, lp['lnv_w'], lp['lnv_b'], ws, bs)


def _attn_group(slope_ref, q_ref, k_ref, v_ref, o_ref, ls_ref, m_ref, dil):
    t = q_ref.shape[0]
    qb = ATTN_BLOCK
    nbk = t // (dil * qb)

    def own_lanes(hh, nrows):
        lane = lax.broadcasted_iota(jnp.int32, (nrows, LANES), 1)
        return lane < HEAD if hh == 0 else lane >= HEAD

    first = own_lanes(0, qb)
    mine = (first, own_lanes(1, qb))

    def rows(start, size):
        return pl.ds(start, size) if dil == 1 else pl.ds(start, size, stride=dil)

    def make_bias(span, shift):
        rel = (shift + lax.broadcasted_iota(jnp.int32, (qb, span), 0)
               - lax.broadcasted_iota(jnp.int32, (qb, span), 1))
        valid = (rel >= 0) & (rel <= N_BACK)
        dist = (rel * dil).astype(F32)
        return [jnp.where(valid, -slope_ref[:, HEAD * hh:HEAD * hh + 1] * dist, MASK_VALUE)
                for hh in range(2)]

    def update(blocks, bias):
        q = [q_ref[qr, :] * (HEAD ** -0.5) for qr, _ in blocks]
        k = [k_ref[kr, :].astype(BF16) for _, kr in blocks]
        v = [v_ref[kr, :] for _, kr in blocks]
        units = [(u, hh) for u in range(len(blocks)) for hh in range(2)]
        s = [_dot(jnp.where(mine[hh], q[u], 0.0), k[u], NT) + bias[hh] for u, hh in units]
        mp = [m_ref[hh, blocks[u][0], :] for u, hh in units]
        rm = [jnp.broadcast_to(jnp.max(x, axis=-1, keepdims=True), (qb, LANES)) for x in s]
        mn = _each(jnp.maximum, mp, rm)
        alpha = _each(lambda a, b: jnp.exp(a - b), mp, mn)
        p = _each(lambda x, m_: jnp.exp(x - jnp.concatenate([m_] * (x.shape[1] // LANES), axis=1)), s, mn)
        vo = [jnp.where(own_lanes(hh, v[u].shape[0]), v[u], 1.0).astype(BF16) for u, hh in units]
        pv = _each(_dot, p, vo)
        done = []
        for u, (qr, _) in enumerate(blocks):
            a0, a1, r0, r1 = alpha[2 * u], alpha[2 * u + 1], pv[2 * u], pv[2 * u + 1]
            acc = o_ref[qr, :] * jnp.where(first, a0, a1) + jnp.where(first, r0, r1)
            den = ls_ref[qr, :] * jnp.where(first, a1, a0) + jnp.where(first, r1, r0)
            done.append((qr, acc, den, mn[2 * u], mn[2 * u + 1]))
        for qr, acc, den, m0, m1 in done:
            o_ref[qr, :] = acc
            ls_ref[qr, :] = den
            m_ref[0, qr, :] = m0
            m_ref[1, qr, :] = m1

    def run(nblocks, rows_fn, bias):
        main = nblocks // ATTN_UNROLL

        def body(width, base):
            def trip(i, carry):
                update([rows_fn(base + i * width + u) for u in range(width)], bias)
                return carry
            return trip

        if main:
            lax.fori_loop(0, main, body(ATTN_UNROLL, 0), 0)
        if nblocks - main * ATTN_UNROLL:
            lax.fori_loop(0, nblocks - main * ATTN_UNROLL, body(1, main * ATTN_UNROLL), 0)

    run(dil, lambda i: (rows(i, qb), rows(i, qb)), make_bias(qb, 0))

    def later(i):
        res = i // (nbk - 1)
        n = 1 + i % (nbk - 1)
        return rows(res + n * (qb * dil), qb), rows(res + (n - 1) * (qb * dil), 2 * qb)

    run(dil * (nbk - 1), later, make_bias(2 * qb, qb))


def _attn_prompt_body(slope_ref, q_ref, k_ref, v_ref, o_ref, ls_ref, m_ref):
    g = pl.program_id(2)

    @pl.when(g == 0)
    def _():
        o_ref[...] = jnp.zeros_like(o_ref)
        ls_ref[...] = jnp.zeros_like(ls_ref)
        m_ref[...] = jnp.full_like(m_ref, MASK_VALUE)

    for gi, dil in enumerate(DILATIONS):
        @pl.when(g == gi)
        def _(dil=dil):
            _attn_group(slope_ref, q_ref, k_ref, v_ref, o_ref, ls_ref, m_ref, dil)

    @pl.when(g == len(DILATIONS) - 1)
    def _():
        o_ref[...] = o_ref[...] / pltpu.roll(ls_ref[...], HEAD, 1)


def _attn_prompt(z_all, col_off, d, slopes):
    b, t, _ = z_all.shape
    assert t % (max(DILATIONS) * 2 * ATTN_BLOCK) == 0
    npair = d // LANES
    base = col_off // LANES
    spec = lambda j: pl.BlockSpec((None, t, LANES),
                                  lambda i, p, g: (i, 0, base + (3 * g + j) * npair + p))
    return pl.pallas_call(
        _attn_prompt_body,
        grid=(b, npair, len(DILATIONS)),
        in_specs=[pl.BlockSpec((None, 1, LANES), lambda i, p, g: (p, 0, 0)), spec(0), spec(1), spec(2)],
        out_specs=pl.BlockSpec((None, t, LANES), lambda i, p, g: (i, 0, p)),
        out_shape=jax.ShapeDtypeStruct((b, t, d), F32),
        scratch_shapes=[pltpu.VMEM((t, LANES), F32), pltpu.VMEM((2, t, LANES), F32)],
        compiler_params=_cparams("parallel", "parallel", "arbitrary"),
        name="attn_prompt",
    )(slopes, z_all, z_all, z_all)


def _attn_sample_body(slope_ref, *refs):
    ng = len(DILATIONS)
    new_refs, newt_refs, cache_refs = refs[:3 * ng], refs[3 * ng:4 * ng], refs[4 * ng:5 * ng]
    o_ref, adv_refs = refs[-ng - 1], refs[-ng:]
    s_len = o_ref.shape[0]
    rows = 2 * s_len
    row = lax.broadcasted_iota(jnp.int32, (rows, LANES), 0)
    lane = lax.broadcasted_iota(jnp.int32, (rows, LANES), 1)
    own = (lane < HEAD) == (row < s_len)
    slope = jnp.where(row[:, :1] < s_len, slope_ref[:, 0:1], slope_ref[:, HEAD:HEAD + 1])
    pad = jnp.zeros((s_len, LANES), F32)
    parts = []
    for gi, dil in enumerate(DILATIONS):
        assert dil & (dil - 1) == 0
        q_ref, kn_ref, vn_ref = new_refs[3 * gi:3 * gi + 3]
        q = q_ref[...] * (HEAD ** -0.5)
        q2 = jnp.where(own, jnp.concatenate([q, q], axis=0), 0.0)
        kt = cache_refs[gi][0]
        vt = cache_refs[gi][1]
        win = kt.shape[1]
        qpos = lax.broadcasted_iota(jnp.int32, (rows, win), 0) % s_len
        gap = qpos + win - lax.broadcasted_iota(jnp.int32, (rows, win), 1)
        ok = (gap <= win) & ((gap & (dil - 1)) == 0)
        sc = jnp.where(ok, _dot(q2, kt) - slope * gap.astype(F32), MASK_VALUE)
        kn = jnp.concatenate([kn_ref[...], pad], axis=0)
        vn = jnp.concatenate([vn_ref[...], pad], axis=0)
        gapn = (lax.broadcasted_iota(jnp.int32, (rows, rows), 0) % s_len
                - lax.broadcasted_iota(jnp.int32, (rows, rows), 1))
        okn = (gapn >= 0) & ((gapn & (dil - 1)) == 0)
        sn = jnp.where(okn, _dot(q2, kn, NT) - slope * gapn.astype(F32), MASK_VALUE)
        m = jnp.maximum(jnp.max(sc, axis=-1, keepdims=True), jnp.max(sn, axis=-1, keepdims=True))
        pc = jnp.exp(sc - m)
        pn = jnp.exp(sn - m)
        den = jnp.sum(pc, axis=-1, keepdims=True) + jnp.sum(pn, axis=-1, keepdims=True)
        num = _dot(pc, vt, NT) + _dot(pn, vn)
        parts.append((m, den, num))
        for j, x in enumerate((kt, vt)):
            adv_refs[gi][j] = pltpu.roll(x, win - s_len, 1)
            adv_refs[gi][j, :, pl.ds(win - s_len, s_len)] = newt_refs[gi][j]
    m_all = functools.reduce(jnp.maximum, [p[0] for p in parts])
    den = sum(p[1] * jnp.exp(p[0] - m_all) for p in parts)
    num = sum(p[2] * jnp.exp(p[0] - m_all) for p in parts)
    o2 = num / den
    o_ref[...] = jnp.where(lane[:s_len] < HEAD, o2[:s_len], o2[s_len:])


def _attn_sample(z_all, col_off, d, layer, caches_t, adv_prev, slopes):
    b, s_len, _ = z_all.shape
    ng = len(DILATIONS)
    npair = d // LANES
    base = col_off // LANES
    new_specs = [pl.BlockSpec((None, s_len, LANES),
                              lambda i, p, g=g, j=j: (i, 0, base + (3 * g + j) * npair + p))
                 for g in range(ng) for j in range(3)]
    newt = [jnp.transpose(z_all[:, :, col_off + (3 * g + 1) * d:col_off + (3 * g + 3) * d]
                          .reshape(b, s_len, 2, d), (0, 2, 3, 1)) for g in range(ng)]
    newt_specs = [pl.BlockSpec((None, 2, LANES, s_len), lambda i, p: (i, 0, p, 0))] * ng
    cache_specs = [pl.BlockSpec((None, None, 2, LANES, c.shape[-1]), lambda i, p: (layer, i, 0, p, 0))
                   for c in caches_t]
    alias_specs = [] if adv_prev is None else [pl.BlockSpec(memory_space=pl.ANY)] * ng
    n_in = 1 + len(new_specs) + 2 * ng
    aliases = {} if adv_prev is None else {n_in + g: 1 + g for g in range(ng)}
    outs = pl.pallas_call(
        _attn_sample_body,
        grid=(b, npair),
        in_specs=[pl.BlockSpec((None, 1, LANES), lambda i, p: (p, 0, 0))] + new_specs + newt_specs
                 + cache_specs + alias_specs,
        out_specs=[pl.BlockSpec((None, s_len, LANES), lambda i, p: (i, 0, p))] + cache_specs,
        out_shape=[jax.ShapeDtypeStruct((b, s_len, d), F32)]
                  + [jax.ShapeDtypeStruct(c.shape, F32) for c in caches_t],
        input_output_aliases=aliases,
        compiler_params=_cparams("parallel", "parallel"),
        name="attn_sample",
    )(slopes, *([z_all] * len(new_specs)), *newt, *caches_t, *(adv_prev or []))
    return outs[0], outs[1:]


def _merge_body(x_ref, oa_ref, ob_ref, oc_ref, zg_ref, bg_ref, wb_ref, wo_ref, o_ref, *, d):
    acc = None
    for n, br in enumerate((oa_ref, ob_ref, oc_ref)):
        sl = slice(n * d, (n + 1) * d)
        term = _dot(br[...], wb_ref[n]) * jax.nn.sigmoid(zg_ref[:, sl] + bg_ref[:, sl])
        acc = term if acc is None else acc + term
    o_ref[...] = x_ref[...] + _dot(acc, wo_ref[...])


def _merge(x, oa, ob, oc, z_all, gate_block, lp, tm):
    m, d = x.shape
    row = pl.BlockSpec((tm, d), lambda i: (i, 0))
    full = lambda a: pl.BlockSpec(a.shape, lambda i: (0,) * a.ndim)
    return pl.pallas_call(
        functools.partial(_merge_body, d=d),
        grid=(m // tm,),
        in_specs=[row, row, row, row, pl.BlockSpec((tm, 3 * d), lambda i: (i, gate_block)),
                  full(lp['b_gate']), full(lp['w_branch']), full(lp['w_out'])],
        out_specs=row,
        out_shape=jax.ShapeDtypeStruct((m, d), F32),
        compiler_params=_cparams("parallel"),
        name="merge",
    )(x, oa, ob, oc, z_all, lp['b_gate'], lp['w_branch'], lp['w_out'])


def _mlp_body(x_ref, g_ref, wu_ref, wd_ref, gf_ref, o_ref, xn_ref, acc_ref, *, final_norm):
    j = pl.program_id(1)

    @pl.when(j == 0)
    def _():
        x = x_ref[...]
        ms = jnp.mean(x * x, axis=-1, keepdims=True)
        xn_ref[...] = (x * lax.rsqrt(ms + NORM_EPS) * g_ref[...]).astype(BF16)
        acc_ref[...] = jnp.zeros_like(acc_ref)

    h = jnp.maximum(jnp.dot(xn_ref[...], wu_ref[...], preferred_element_type=F32), 0.0)
    acc_ref[...] += _dot(h * h, wd_ref[...])

    @pl.when(j == pl.num_programs(1) - 1)
    def _():
        y = x_ref[...] + acc_ref[...]
        if final_norm:
            ms = jnp.mean(y * y, axis=-1, keepdims=True)
            y = y * lax.rsqrt(ms + NORM_EPS) * gf_ref[...]
        o_ref[...] = y


def _mlp(x, lp, g_final, final_norm, tm, tf):
    m, d = x.shape
    ff = lp['w_up'].shape[1]
    vec = pl.BlockSpec((1, d), lambda i, j: (0, 0))
    row = pl.BlockSpec((tm, d), lambda i, j: (i, 0))
    return pl.pallas_call(
        functools.partial(_mlp_body, final_norm=final_norm),
        grid=(m // tm, ff // tf),
        in_specs=[row, vec, pl.BlockSpec((d, tf), lambda i, j: (0, j)),
                  pl.BlockSpec((tf, d), lambda i, j: (j, 0)), vec],
        out_specs=row,
        out_shape=jax.ShapeDtypeStruct((m, d), F32),
        scratch_shapes=[pltpu.VMEM((tm, d), BF16), pltpu.VMEM((tm, d), F32)],
        compiler_params=_cparams("parallel", "arbitrary"),
        name="mlp",
    )(x, lp['ln_mlp'], lp['w_up'], lp['w_down'], g_final)


def _pad_cols(a, width):
    return jnp.pad(a, [(0, 0)] * (a.ndim - 1) + [(0, width - a.shape[-1])])


def _pad_rows(a, height):
    return jnp.pad(a, [(0, height - a.shape[0]), (0, 0)])


def _rwkv_cols(a, d):
    e = 3 * d
    pieces = [a[..., :e], _pad_cols(a[..., e:e + LORA_W], LORA_PAD[0]),
              _pad_cols(a[..., e + LORA_W:e + LORA_W + LORA_A], LORA_PAD[1]),
              _pad_cols(a[..., e + LORA_W + LORA_A:], LORA_PAD[2])]
    return jnp.concatenate(pieces, axis=-1)


def _rwkv_cols_inverse(a, d):
    e = 3 * d
    o1 = e + LORA_PAD[0]
    o2 = o1 + LORA_PAD[1]
    return jnp.concatenate([a[..., :e], a[..., e:e + LORA_W], a[..., o1:o1 + LORA_A],
                            a[..., o2:o2 + LORA_G]], axis=-1)


def _layer_params(l, d, ln_mix, w_in, mu_shift, decay_w0, decay_w2, iclr_a0, iclr_a2, gate_g2, k_k, k_a,
                  r_k, lnx_w, lnx_b, lnv_w, lnv_b, w_s, b_s, b_gate, w_branch, w_out, ln_mlp, w_up, w_down):
    shift_w = 3 * d + LORA_W + LORA_A + LORA_G
    e1 = shift_w
    e2 = e1 + 2 * d
    e3 = e2 + 9 * d
    w = w_in[l]
    w_perm = jnp.concatenate([w[:, e2:e3], w[:, e3:], w[:, e1:e2], _rwkv_cols(w[:, :e1], d)], axis=1)
    row = lambda a: a[l].reshape(1, -1)
    return dict(
        ln_mix=row(ln_mix), w_in=w_perm.astype(BF16), mu_shift=_rwkv_cols(row(mu_shift), d),
        decay_w0=row(decay_w0), decay_w2=_pad_rows(decay_w2[l], LORA_PAD[0]).astype(BF16),
        iclr_a0=row(iclr_a0), iclr_a2=_pad_rows(iclr_a2[l], LORA_PAD[1]).astype(BF16),
        gate_g2=_pad_rows(gate_g2[l], LORA_PAD[2]).astype(BF16),
        k_k=row(k_k), k_a=row(k_a), r_k=row(r_k), lnx_w=row(lnx_w), lnx_b=row(lnx_b),
        lnv_w=row(lnv_w), lnv_b=row(lnv_b), w_s=w_s[l],
        b_s_cols=jnp.repeat(b_s[l].T, d // GMLP_GROUPS, axis=1),
        b_gate=row(b_gate), w_branch=w_branch[l].astype(BF16), w_out=w_out[l].astype(BF16),
        ln_mlp=row(ln_mlp), w_up=w_up[l].astype(BF16), w_down=w_down[l].astype(BF16))


def _tile(n, pref):
    while n % pref:
        pref //= 2
    return pref


def _layer(x, lp, g_final, final_norm, shift_prev, wkv_prev, kv_state, slopes_pair):
    b, t, d = x.shape
    m = b * t
    wr = 3 * d + sum(LORA_PAD)
    off_gate, off_gmlp, off_rwkv = 9 * d, 12 * d, 14 * d
    x2 = x.reshape(m, d)
    z_all = _rms_matmul(x2, lp['ln_mix'], lp['w_in'], _tile(m, 1024), 512).reshape(b, t, -1)

    sp = _rwkv_cols(shift_prev, d)[:, None, :]
    parts = _rwkv_prep(z_all, off_rwkv // wr, sp, lp, _tile(t, 128))
    o_a, wkv = _wkv_chunk(parts, lp, wkv_prev, _tile(t, WKV_CHUNK), 16)
    shift_last = _rwkv_cols_inverse(z_all[:, t - 1, off_rwkv:], d)

    if kv_state is None:
        (o_b,), v_rows = _gmlp(z_all, off_gmlp // (2 * d), lp, GMLP_CHUNK, False), None
        o_c = _attn_prompt(z_all, 0, d, slopes_pair)
        kv_out = []
        for gi, win in enumerate(WINDOWS):
            keep = min(win, t)
            kv = z_all[:, t - keep:, (3 * gi + 1) * d:(3 * gi + 3) * d].reshape(b, keep, 2, d // HEAD, HEAD)
            kv_out.append(jnp.swapaxes(kv, 1, 2))
    else:
        o_b, v_rows = _gmlp(z_all, off_gmlp // (2 * d), lp, t, True)
        o_c, kv_out = _attn_sample(z_all, 0, d, *kv_state, slopes_pair)

    hmid = _merge(x2, o_a.reshape(m, d), o_b.reshape(m, d), o_c.reshape(m, d),
                  z_all.reshape(m, -1), off_gate // (3 * d), lp, _tile(m, 256))
    out = _mlp(hmid, lp, g_final, final_norm, _tile(m, 512), 1024)
    return out.reshape(b, t, d), shift_last, wkv, kv_out, v_rows


def kernel(x_prompt, x_sample, state_wkv, state_shift, cache_kv_w128, cache_kv_w512, cache_kv_w2048,
           ln_mix, w_in, mu_shift, decay_w0, decay_w2, iclr_a0, iclr_a2, gate_g2, k_k, k_a, r_k,
           lnx_w, lnx_b, lnv_w, lnv_b, w_s, b_s, b_gate, w_branch, w_out, ln_mlp, w_up, w_down,
           ln_final):
    depth = w_in.shape[0]
    d = x_prompt.shape[-1]
    nh = d // HEAD
    nb_prompt = x_prompt.shape[0]
    shift_w = state_shift.shape[-1]
    slopes = jnp.exp2(-8.0 * jnp.arange(1, nh + 1, dtype=F32) / nh)
    slopes_pair = jnp.repeat(slopes, HEAD).reshape(nh // 2, 1, LANES)
    g_final = ln_final.reshape(1, d)
    caches_t = [jnp.transpose(c, (0, 1, 2, 4, 5, 3)).reshape(c.shape[:3] + (d, c.shape[3]))
                for c in (cache_kv_w128, cache_kv_w512, cache_kv_w2048)]

    xp, xs = x_prompt, x_sample
    outs_p, outs_s = [], []
    adv = None
    for l in range(depth):
        lp = _layer_params(l, d, ln_mix, w_in, mu_shift, decay_w0, decay_w2, iclr_a0, iclr_a2, gate_g2,
                           k_k, k_a, r_k, lnx_w, lnx_b, lnv_w, lnv_b, w_s, b_s, b_gate, w_branch, w_out,
                           ln_mlp, w_up, w_down)
        last = l == depth - 1
        xp, sh, wk, kv, _ = _layer(xp, lp, g_final, last, jnp.zeros((nb_prompt, shift_w), F32),
                                   jnp.zeros((nb_prompt, nh, HEAD, HEAD), F32), None, slopes_pair)
        outs_p.append((wk, sh, kv))
        xs, sh, wk, adv, vr = _layer(xs, lp, g_final, last, state_shift[l], state_wkv[l],
                                     (l, caches_t, adv), slopes_pair)
        outs_s.append((wk, sh, vr))

    stack = lambda f, outs: jnp.stack([f(o) for o in outs])
    s_kv = [jnp.transpose(c.reshape(c.shape[:3] + (nh, HEAD, c.shape[-1])), (0, 1, 2, 5, 3, 4)) for c in adv]
    return (xp, xs,
            stack(lambda o: o[0], outs_p), stack(lambda o: o[1], outs_p),
            stack(lambda o: o[2][0], outs_p), stack(lambda o: o[2][1], outs_p),
            stack(lambda o: o[2][2], outs_p),
            stack(lambda o: o[0], outs_s), stack(lambda o: o[1], outs_s),
            s_kv[0], s_kv[1], s_kv[2], stack(lambda o: o[2], outs_s))
```

```python
import functools

import jax
import jax.numpy as jnp
from jax import lax
from jax.experimental import pallas as pl
from jax.experimental.pallas import tpu as pltpu

F32 = jnp.float32
BF16 = jnp.bfloat16

HEAD = 64
LORA_W, LORA_A, LORA_G = 64, 64, 160
GMLP_CHUNK = 128
GMLP_GROUPS = 8
WINDOWS = (128, 512, 2048)
DILATIONS = (1, 4, 16)
N_BACK = 128
ATTN_BLOCK = 128
ATTN_UNROLL = 4
NORM_EPS = 1e-6
LN_EPS = 1e-5
GN_EPS = 64e-5
MASK_VALUE = -1e30

LANES = 128
VMEM_LIMIT_BYTES = 48 * 2**20

LORA_PAD = (128, 128, 256)
WKV_CHUNK = 64

NN = ((1,), (0,))
NT = ((1,), (1,))
TN = ((0,), (0,))


def _cparams(*sem):
    return pltpu.CompilerParams(dimension_semantics=sem, vmem_limit_bytes=VMEM_LIMIT_BYTES)


def _dot(a, b, dims=NN):
    return lax.dot_general(a.astype(BF16), b.astype(BF16), (dims, ((), ())),
                           preferred_element_type=F32)


def _rms_matmul_body(x_ref, g_ref, w_ref, o_ref, xn_ref):
    @pl.when(pl.program_id(1) == 0)
    def _():
        x = x_ref[...]
        ms = jnp.mean(x * x, axis=-1, keepdims=True)
        xn_ref[...] = (x * lax.rsqrt(ms + NORM_EPS) * g_ref[...]).astype(BF16)

    o_ref[...] = jnp.dot(xn_ref[...], w_ref[...], preferred_element_type=F32).astype(o_ref.dtype)


def _rms_matmul(x, g, w, tm, tn, out_dtype):
    m, d = x.shape
    n = w.shape[1]
    return pl.pallas_call(
        _rms_matmul_body,
        grid=(m // tm, n // tn),
        in_specs=[pl.BlockSpec((tm, d), lambda i, j: (i, 0)),
                  pl.BlockSpec((1, d), lambda i, j: (0, 0)),
                  pl.BlockSpec((d, tn), lambda i, j: (0, j))],
        out_specs=pl.BlockSpec((tm, tn), lambda i, j: (i, j)),
        out_shape=jax.ShapeDtypeStruct((m, n), out_dtype),
        scratch_shapes=[pltpu.VMEM((tm, d), BF16)],
        compiler_params=_cparams("parallel", "arbitrary"),
        name="rms_matmul",
    )(x, g, w)


def _rwkv_prep_body(z_ref, sp_ref, mu_ref, w0_ref, dw2_ref, a0_ref, a2_ref, g2_ref, kk_ref, ka_ref,
                    r_ref, lw_ref, k2_ref, v_ref, kku_ref, a_ref, g_ref, carry_ref, *, d):
    @pl.when(pl.program_id(1) == 0)
    def _():
        carry_ref[...] = sp_ref[...]

    zr = z_ref[...].astype(F32)
    tt = zr.shape[0]
    row = lax.broadcasted_iota(jnp.int32, zr.shape, 0)
    prev = jnp.where(row == 0, carry_ref[...], pltpu.roll(zr, 1, 0))
    carry_ref[...] = zr[tt - 1:tt, :]
    h = zr + (prev - zr) * mu_ref[...]
    e_w = 3 * d
    e_a = e_w + LORA_PAD[0]
    e_g = e_a + LORA_PAD[1]
    k = h[:, d:2 * d]
    xw = w0_ref[...] + _dot(jnp.tanh(h[:, e_w:e_a]), dw2_ref[...])
    softplus = jnp.maximum(-xw, 0.0) + jnp.log1p(jnp.exp(-jnp.abs(xw)))
    a = jax.nn.sigmoid(a0_ref[...] + _dot(h[:, e_a:e_g], a2_ref[...]))
    r_ref[...] = h[:, :d]
    lw_ref[...] = -jnp.exp(-softplus - 0.5)
    k2_ref[...] = k * (1.0 + (a - 1.0) * ka_ref[...])
    v_ref[...] = h[:, 2 * d:3 * d]
    kku_ref[...] = k * kk_ref[...]
    a_ref[...] = a
    g_ref[...] = _dot(jax.nn.sigmoid(h[:, e_g:]), g2_ref[...])


def _rwkv_prep(z_all, col_block, shift_prev, lp, tt):
    b, t, _ = z_all.shape
    d = lp['k_k'].shape[-1]
    wr = 3 * d + sum(LORA_PAD)
    full = lambda a: pl.BlockSpec(a.shape, lambda i, j: (0,) * a.ndim)
    params = [lp['mu_shift'], lp['decay_w0'], lp['decay_w2'], lp['iclr_a0'], lp['iclr_a2'],
              lp['gate_g2'], lp['k_k'], lp['k_a']]
    out_spec = pl.BlockSpec((None, tt, d), lambda i, j: (i, j, 0))
    return pl.pallas_call(
        functools.partial(_rwkv_prep_body, d=d),
        grid=(b, t // tt),
        in_specs=[pl.BlockSpec((None, tt, wr), lambda i, j: (i, j, col_block)),
                  pl.BlockSpec((None, 1, wr), lambda i, j: (i, 0, 0))] + [full(p) for p in params],
        out_specs=[out_spec] * 7,
        out_shape=[jax.ShapeDtypeStruct((b, t, d), F32)] * 7,
        scratch_shapes=[pltpu.VMEM((1, wr), F32)],
        compiler_params=_cparams("parallel", "arbitrary"),
        name="rwkv_prep",
    )(z_all, shift_prev, *params)


def _each(f, *lists):
    return [f(*args) for args in zip(*lists)]


def _unit_lower_solvers(mats):
    c = mats[0].shape[0]
    bs = min(c, 16)
    nb = c // bs
    assert nb in (1, 2, 4)
    ri = lax.broadcasted_iota(jnp.int32, (c, c), 0)
    ci = lax.broadcasted_iota(jnp.int32, (c, c), 1)
    same = (ri // bs) == (ci // bs)
    eye = jnp.where(ri == ci, 1.0, 0.0)
    pw = [jnp.where(same, a, 0.0) for a in mats]
    tinv = [eye - d for d in pw]
    n = 2
    while n < bs:
        pw = _each(_dot, pw, pw)
        tinv = _each(lambda t, x: t + _dot(t, x), tinv, pw)
        n *= 2
    m = _each(lambda t, a: _dot(t, jnp.where(same, 0.0, a)), tinv, mats) if nb > 1 else None

    def solve(rhs):
        x = _each(_dot, tinv, rhs)
        if nb > 2:
            mx = _each(_dot, m, x)
            x = _each(lambda x_, m_, y_: x_ + _dot(m_, y_), x, m, mx)
        if nb > 1:
            x = _each(lambda x_, m_: x_ - _dot(m_, x_), x, m)
        return x

    return solve


def _cumsum_rows(x):
    c = x.shape[0]
    row = lax.broadcasted_iota(jnp.int32, x.shape, 0)
    s = 1
    while s < c:
        x = x + jnp.where(row >= s, pltpu.roll(x, s, 0), 0.0)
        s *= 2
    return x


def _wkv_chunk_body(r_ref, lw_ref, k2_ref, v_ref, kku_ref, a_ref, g_ref, lnw_ref, lnb_ref, rk_ref,
                    s0_ref, o_ref, s_ref, *, hb):
    @pl.when(pl.program_id(2) == 0)
    def _():
        s_ref[...] = s0_ref[...]

    c = r_ref.shape[0]
    ri = lax.broadcasted_iota(jnp.int32, (c, c), 0)
    ci = lax.broadcasted_iota(jnp.int32, (c, c), 1)
    lower = ri >= ci
    strict = ri > ci
    ek = jnp.where(lax.broadcasted_iota(jnp.int32, (HEAD, HEAD), 0)
                   == lax.broadcasted_iota(jnp.int32, (HEAD, HEAD), 1), 1.0, 0.0)

    lw = lw_ref[...]
    cum = _cumsum_rows(lw)
    gam = jnp.exp(cum)
    gam_prev = jnp.exp(cum - lw)
    igam = jnp.exp(-cum)
    tail = jnp.exp(cum[c - 1:c, :] - cum)

    sls = [slice(HEAD * h, HEAD * (h + 1)) for h in range(hb)]
    cols = lambda x: [x[:, sl] for sl in sls]
    r, k2, v, kku, av = cols(r_ref), cols(k2_ref), cols(v_ref), cols(kku_ref), cols(a_ref)
    kk = [x / jnp.maximum(jnp.sqrt(jnp.sum(x * x, axis=-1, keepdims=True)), 1e-12) for x in kku]
    mul = lambda x, y: x * y
    bvec = _each(mul, kk, av)
    kt = _each(mul, kk, cols(gam_prev))
    rt = _each(mul, r, cols(gam))
    kh = _each(mul, k2, cols(igam))
    bh = _each(mul, bvec, cols(igam))
    kb = _each(mul, k2, cols(tail))
    bb = _each(mul, bvec, cols(tail))

    lhs = _each(lambda x, y: jnp.concatenate([x, y], axis=0), kt, rt)
    sb = _each(lambda x, y: _dot(x, y, NT), lhs, bh)
    sk = _each(lambda x, y: _dot(x, y, NT), lhs, kh)
    a_ub = [jnp.where(strict, x[:c], 0.0) for x in sb]
    a_vk = [jnp.where(strict, x[:c], 0.0) for x in sk]
    b_ub = [jnp.where(lower, x[c:], 0.0) for x in sb]
    b_vk = [jnp.where(lower, x[c:], 0.0) for x in sk]

    solve = _unit_lower_solvers(a_ub)
    av_v = _each(_dot, a_vk, v)
    sol = solve(_each(lambda x, y: jnp.concatenate([x, y], axis=1), kt, av_v))
    bu = _each(_dot, b_ub, sol)
    bv_v = _each(_dot, b_vk, v)
    w_ = [x[:, :HEAD] for x in sol]
    u0 = [x[:, HEAD:] for x in sol]
    q = _each(lambda x, y: x - y[:, :HEAD], rt, bu)
    y0 = _each(lambda x, y: x - y[:, HEAD:], bv_v, bu)
    s_in = [s_ref[h] for h in range(hb)]
    y = _each(lambda q_, s_, y_: _dot(q_, s_, NT) + y_, q, s_in, y0)
    p = _each(lambda w, b_, sl: ek * gam[c - 1:c, sl] - _dot(w, b_, TN), w_, bb, sls)
    z = _each(lambda v_, u_, kb_, bb_: _dot(jnp.concatenate([v_, -u_], axis=0),
                                            jnp.concatenate([kb_, bb_], axis=0), TN), v, u0, kb, bb)
    s_new = _each(lambda s_, p_, z_: _dot(s_, p_) + z_, s_in, p, z)
    for h in range(hb):
        s_ref[h] = s_new[h]

    mu = [jnp.mean(x, axis=-1, keepdims=True) for x in y]
    yc = _each(lambda x, m_: x - m_, y, mu)
    var = [jnp.mean(x * x, axis=-1, keepdims=True) for x in yc]
    for h, sl in enumerate(sls):
        yn = yc[h] * lax.rsqrt(var[h] + GN_EPS) * lnw_ref[:, sl] + lnb_ref[:, sl]
        bonus = jnp.sum(r[h] * k2[h] * rk_ref[:, sl], axis=-1, keepdims=True) * v[h]
        o_ref[:, sl] = (yn + bonus) * g_ref[:, sl]


def _wkv_chunk(parts, lp, s0, chunk, hb):
    b, t, d = parts[0].shape
    nh = d // HEAD
    wb = hb * HEAD
    tok = pl.BlockSpec((None, chunk, wb), lambda i, j, c: (i, c, j))
    par = pl.BlockSpec((1, wb), lambda i, j, c: (0, j))
    st = pl.BlockSpec((None, hb, HEAD, HEAD), lambda i, j, c: (i, j, 0, 0))
    return pl.pallas_call(
        functools.partial(_wkv_chunk_body, hb=hb),
        grid=(b, nh // hb, t // chunk),
        in_specs=[tok] * 7 + [par] * 3 + [st],
        out_specs=[tok, st],
        out_shape=[jax.ShapeDtypeStruct((b, t, d), F32),
                   jax.ShapeDtypeStruct((b, nh, HEAD, HEAD), F32)],
        compiler_params=_cparams("parallel", "parallel", "arbitrary"),
        name="wkv_chunk",
    )(*parts, lp['lnx_w'], lp['lnx_b'], lp['r_k'], s0)


def _gmlp_body(z_ref, lnw_ref, lnb_ref, ws_ref, bs_ref, o_ref, *vn_out, d):
    z = z_ref[...].astype(F32)
    z = 0.5 * z * (1.0 + lax.erf(z * (2.0 ** -0.5)))
    u, vv = z[:, :d], z[:, d:]
    mu = jnp.mean(vv, axis=-1, keepdims=True)
    vc = vv - mu
    var = jnp.mean(vc * vc, axis=-1, keepdims=True)
    vn = vc * lax.rsqrt(var + LN_EPS) * lnw_ref[...] + lnb_ref[...]
    if vn_out:
        vn_out[0][...] = vn
    c = z.shape[0]
    causal = (lax.broadcasted_iota(jnp.int32, (c, c), 0) >= lax.broadcasted_iota(jnp.int32, (c, c), 1))
    cg = d // GMLP_GROUPS
    for g in range(GMLP_GROUPS):
        sl = slice(cg * g, cg * (g + 1))
        mixed = _dot(jnp.where(causal, ws_ref[g], 0.0), vn[:, sl])
        o_ref[:, sl] = u[:, sl] * (mixed + bs_ref[:, sl])


def _gmlp(z_all, col_block, lp, chunk, want_vn):
    b, t, _ = z_all.shape
    d = lp['lnv_w'].shape[-1]
    ws, bs = lp['w_s'][:, :chunk, :chunk], lp['b_s_cols'][:chunk]
    full = lambda a: pl.BlockSpec(a.shape, lambda i, j: (0,) * a.ndim)
    out_spec = pl.BlockSpec((None, chunk, d), lambda i, j: (i, j, 0))
    n_out = 2 if want_vn else 1
    return pl.pallas_call(
        functools.partial(_gmlp_body, d=d),
        grid=(b, t // chunk),
        in_specs=[pl.BlockSpec((None, chunk, 2 * d), lambda i, j: (i, j, col_block)),
                  full(lp['lnv_w']), full(lp['lnv_b']), full(ws), full(bs)],
        out_specs=[out_spec] * n_out,
        out_shape=[jax.ShapeDtypeStruct((b, t, d), F32)] * n_out,
        compiler_params=_cparams("parallel", "parallel"),
        name="gmlp",
    )(z_all, lp['lnv_w'], lp['lnv_b'], ws, bs)


def _attn_group(slope_ref, q_ref, k_ref, v_ref, o_ref, ls_ref, m_ref, dil):
    t = q_ref.shape[0]
    qb = ATTN_BLOCK
    nbk = t // (dil * qb)

    def own_lanes(hh, nrows):
        lane = lax.broadcasted_iota(jnp.int32, (nrows, LANES), 1)
        return lane < HEAD if hh == 0 else lane >= HEAD

    first = own_lanes(0, qb)
    mine = (first, own_lanes(1, qb))

    def rows(start, size):
        return pl.ds(start, size) if dil == 1 else pl.ds(start, size, stride=dil)

    def make_bias(span, shift):
        rel = (shift + lax.broadcasted_iota(jnp.int32, (qb, span), 0)
               - lax.broadcasted_iota(jnp.int32, (qb, span), 1))
        valid = (rel >= 0) & (rel <= N_BACK)
        dist = (rel * dil).astype(F32)
        return [jnp.where(valid, -slope_ref[:, HEAD * hh:HEAD * hh + 1] * dist, MASK_VALUE)
                for hh in range(2)]

    def update(blocks, bias):
        q = [q_ref[qr, :] * (HEAD ** -0.5) for qr, _ in blocks]
        k = [k_ref[kr, :].astype(BF16) for _, kr in blocks]
        v = [v_ref[kr, :] for _, kr in blocks]
        units = [(u, hh) for u in range(len(blocks)) for hh in range(2)]
        s = [_dot(jnp.where(mine[hh], q[u], 0.0), k[u], NT) + bias[hh] for u, hh in units]
        mp = [m_ref[hh, blocks[u][0], :] for u, hh in units]
        rm = [jnp.broadcast_to(jnp.max(x, axis=-1, keepdims=True), (qb, LANES)) for x in s]
        mn = _each(jnp.maximum, mp, rm)
        alpha = _each(lambda a, b: jnp.exp(a - b), mp, mn)
        p = _each(lambda x, m_: jnp.exp(x - jnp.concatenate([m_] * (x.shape[1] // LANES), axis=1)), s, mn)
        vo = [jnp.where(own_lanes(hh, v[u].shape[0]), v[u], 1.0).astype(BF16) for u, hh in units]
        pv = _each(_dot, p, vo)
        done = []
        for u, (qr, _) in enumerate(blocks):
            a0, a1, r0, r1 = alpha[2 * u], alpha[2 * u + 1], pv[2 * u], pv[2 * u + 1]
            acc = o_ref[qr, :] * jnp.where(first, a0, a1) + jnp.where(first, r0, r1)
            den = ls_ref[qr, :] * jnp.where(first, a1, a0) + jnp.where(first, r1, r0)
            done.append((qr, acc, den, mn[2 * u], mn[2 * u + 1]))
        for qr, acc, den, m0, m1 in done:
            o_ref[qr, :] = acc
            ls_ref[qr, :] = den
            m_ref[0, qr, :] = m0
            m_ref[1, qr, :] = m1

    def run(nblocks, rows_fn, bias):
        main = nblocks // ATTN_UNROLL

        def body(width, base):
            def trip(i, carry):
                update([rows_fn(base + i * width + u) for u in range(width)], bias)
                return carry
            return trip

        if main:
            lax.fori_loop(0, main, body(ATTN_UNROLL, 0), 0)
        if nblocks - main * ATTN_UNROLL:
            lax.fori_loop(0, nblocks - main * ATTN_UNROLL, body(1, main * ATTN_UNROLL), 0)

    run(dil, lambda i: (rows(i, qb), rows(i, qb)), make_bias(qb, 0))

    def later(i):
        res = i // (nbk - 1)
        n = 1 + i % (nbk - 1)
        return rows(res + n * (qb * dil), qb), rows(res + (n - 1) * (qb * dil), 2 * qb)

    run(dil * (nbk - 1), later, make_bias(2 * qb, qb))


def _attn_prompt_body(slope_ref, q_ref, k_ref, v_ref, o_ref, ls_ref, m_ref):
    g = pl.program_id(2)

    @pl.when(g == 0)
    def _():
        o_ref[...] = jnp.zeros_like(o_ref)
        ls_ref[...] = jnp.zeros_like(ls_ref)
        m_ref[...] = jnp.full_like(m_ref, MASK_VALUE)

    for gi, dil in enumerate(DILATIONS):
        @pl.when(g == gi)
        def _(dil=dil):
            _attn_group(slope_ref, q_ref, k_ref, v_ref, o_ref, ls_ref, m_ref, dil)

    @pl.when(g == len(DILATIONS) - 1)
    def _():
        o_ref[...] = o_ref[...] / pltpu.roll(ls_ref[...], HEAD, 1)


def _attn_prompt(z_all, col_off, d, slopes):
    b, t, _ = z_all.shape
    assert t % (max(DILATIONS) * 2 * ATTN_BLOCK) == 0
    npair = d // LANES
    base = col_off // LANES
    spec = lambda j: pl.BlockSpec((None, t, LANES),
                                  lambda i, p, g: (i, 0, base + (3 * g + j) * npair + p))
    return pl.pallas_call(
        _attn_prompt_body,
        grid=(b, npair, len(DILATIONS)),
        in_specs=[pl.BlockSpec((None, 1, LANES), lambda i, p, g: (p, 0, 0)), spec(0), spec(1), spec(2)],
        out_specs=pl.BlockSpec((None, t, LANES), lambda i, p, g: (i, 0, p)),
        out_shape=jax.ShapeDtypeStruct((b, t, d), F32),
        scratch_shapes=[pltpu.VMEM((t, LANES), F32), pltpu.VMEM((2, t, LANES), F32)],
        compiler_params=_cparams("parallel", "parallel", "arbitrary"),
        name="attn_prompt",
    )(slopes, z_all, z_all, z_all)


def _attn_sample_body(slope_ref, *refs):
    ng = len(DILATIONS)
    new_refs, newt_refs, cache_refs = refs[:3 * ng], refs[3 * ng:4 * ng], refs[4 * ng:5 * ng]
    o_ref, adv_refs = refs[-ng - 1], refs[-ng:]
    s_len = o_ref.shape[0]
    rows = 2 * s_len
    row = lax.broadcasted_iota(jnp.int32, (rows, LANES), 0)
    lane = lax.broadcasted_iota(jnp.int32, (rows, LANES), 1)
    own = (lane < HEAD) == (row < s_len)
    slope = jnp.where(row[:, :1] < s_len, slope_ref[:, 0:1], slope_ref[:, HEAD:HEAD + 1])
    pad = jnp.zeros((s_len, LANES), F32)
    parts = []
    for gi, dil in enumerate(DILATIONS):
        assert dil & (dil - 1) == 0
        q_ref, kn_ref, vn_ref = new_refs[3 * gi:3 * gi + 3]
        q = q_ref[...] * (HEAD ** -0.5)
        q2 = jnp.where(own, jnp.concatenate([q, q], axis=0), 0.0)
        kt = cache_refs[gi][0]
        vt = cache_refs[gi][1]
        win = kt.shape[1]
        qpos = lax.broadcasted_iota(jnp.int32, (rows, win), 0) % s_len
        gap = qpos + win - lax.broadcasted_iota(jnp.int32, (rows, win), 1)
        ok = (gap <= win) & ((gap & (dil - 1)) == 0)
        sc = jnp.where(ok, _dot(q2, kt) - slope * gap.astype(F32), MASK_VALUE)
        kn = jnp.concatenate([kn_ref[...], pad], axis=0)
        vn = jnp.concatenate([vn_ref[...], pad], axis=0)
        gapn = (lax.broadcasted_iota(jnp.int32, (rows, rows), 0) % s_len
                - lax.broadcasted_iota(jnp.int32, (rows, rows), 1))
        okn = (gapn >= 0) & ((gapn & (dil - 1)) == 0)
        sn = jnp.where(okn, _dot(q2, kn, NT) - slope * gapn.astype(F32), MASK_VALUE)
        m = jnp.maximum(jnp.max(sc, axis=-1, keepdims=True), jnp.max(sn, axis=-1, keepdims=True))
        pc = jnp.exp(sc - m)
        pn = jnp.exp(sn - m)
        den = jnp.sum(pc, axis=-1, keepdims=True) + jnp.sum(pn, axis=-1, keepdims=True)
        num = _dot(pc, vt, NT) + _dot(pn, vn)
        parts.append((m, den, num))
        for j, x in enumerate((kt, vt)):
            adv_refs[gi][j] = pltpu.roll(x, win - s_len, 1)
            adv_refs[gi][j, :, pl.ds(win - s_len, s_len)] = newt_refs[gi][j]
    m_all = functools.reduce(jnp.maximum, [p[0] for p in parts])
    den = sum(p[1] * jnp.exp(p[0] - m_all) for p in parts)
    num = sum(p[2] * jnp.exp(p[0] - m_all) for p in parts)
    o2 = num / den
    o_ref[...] = jnp.where(lane[:s_len] < HEAD, o2[:s_len], o2[s_len:])


def _attn_sample(z_all, col_off, d, layer, caches_t, adv_prev, slopes):
    b, s_len, _ = z_all.shape
    ng = len(DILATIONS)
    npair = d // LANES
    base = col_off // LANES
    new_specs = [pl.BlockSpec((None, s_len, LANES),
                              lambda i, p, g=g, j=j: (i, 0, base + (3 * g + j) * npair + p))
                 for g in range(ng) for j in range(3)]
    newt = [jnp.transpose(z_all[:, :, col_off + (3 * g + 1) * d:col_off + (3 * g + 3) * d]
                          .reshape(b, s_len, 2, d), (0, 2, 3, 1)) for g in range(ng)]
    newt_specs = [pl.BlockSpec((None, 2, LANES, s_len), lambda i, p: (i, 0, p, 0))] * ng
    cache_specs = [pl.BlockSpec((None, None, 2, LANES, c.shape[-1]), lambda i, p: (layer, i, 0, p, 0))
                   for c in caches_t]
    alias_specs = [] if adv_prev is None else [pl.BlockSpec(memory_space=pl.ANY)] * ng
    n_in = 1 + len(new_specs) + 2 * ng
    aliases = {} if adv_prev is None else {n_in + g: 1 + g for g in range(ng)}
    outs = pl.pallas_call(
        _attn_sample_body,
        grid=(b, npair),
        in_specs=[pl.BlockSpec((None, 1, LANES), lambda i, p: (p, 0, 0))] + new_specs + newt_specs
                 + cache_specs + alias_specs,
        out_specs=[pl.BlockSpec((None, s_len, LANES), lambda i, p: (i, 0, p))] + cache_specs,
        out_shape=[jax.ShapeDtypeStruct((b, s_len, d), F32)]
                  + [jax.ShapeDtypeStruct(c.shape, F32) for c in caches_t],
        input_output_aliases=aliases,
        compiler_params=_cparams("parallel", "parallel"),
        name="attn_sample",
    )(slopes, *([z_all] * len(new_specs)), *newt, *caches_t, *(adv_prev or []))
    return outs[0], outs[1:]


def _merge_body(x_ref, oa_ref, ob_ref, oc_ref, zg_ref, bg_ref, wb_ref, wo_ref, o_ref, *, d):
    acc = None
    for n, br in enumerate((oa_ref, ob_ref, oc_ref)):
        sl = slice(n * d, (n + 1) * d)
        term = _dot(br[...], wb_ref[n]) * jax.nn.sigmoid(zg_ref[:, sl].astype(F32) + bg_ref[:, sl])
        acc = term if acc is None else acc + term
    o_ref[...] = x_ref[...] + _dot(acc, wo_ref[...])


def _merge(x, oa, ob, oc, z_all, gate_block, lp, tm):
    m, d = x.shape
    row = pl.BlockSpec((tm, d), lambda i: (i, 0))
    full = lambda a: pl.BlockSpec(a.shape, lambda i: (0,) * a.ndim)
    return pl.pallas_call(
        functools.partial(_merge_body, d=d),
        grid=(m // tm,),
        in_specs=[row, row, row, row, pl.BlockSpec((tm, 3 * d), lambda i: (i, gate_block)),
                  full(lp['b_gate']), full(lp['w_branch']), full(lp['w_out'])],
        out_specs=row,
        out_shape=jax.ShapeDtypeStruct((m, d), F32),
        compiler_params=_cparams("parallel"),
        name="merge",
    )(x, oa, ob, oc, z_all, lp['b_gate'], lp['w_branch'], lp['w_out'])


def _mlp_body(x_ref, g_ref, wu_ref, wd_ref, gf_ref, o_ref, xn_ref, acc_ref, *, final_norm):
    j = pl.program_id(1)

    @pl.when(j == 0)
    def _():
        x = x_ref[...]
        ms = jnp.mean(x * x, axis=-1, keepdims=True)
        xn_ref[...] = (x * lax.rsqrt(ms + NORM_EPS) * g_ref[...]).astype(BF16)
        acc_ref[...] = jnp.zeros_like(acc_ref)

    h = jnp.maximum(jnp.dot(xn_ref[...], wu_ref[...], preferred_element_type=F32), 0.0)
    acc_ref[...] += _dot(h * h, wd_ref[...])

    @pl.when(j == pl.num_programs(1) - 1)
    def _():
        y = x_ref[...] + acc_ref[...]
        if final_norm:
            ms = jnp.mean(y * y, axis=-1, keepdims=True)
            y = y * lax.rsqrt(ms + NORM_EPS) * gf_ref[...]
        o_ref[...] = y


def _mlp(x, lp, g_final, final_norm, tm, tf):
    m, d = x.shape
    ff = lp['w_up'].shape[1]
    vec = pl.BlockSpec((1, d), lambda i, j: (0, 0))
    row = pl.BlockSpec((tm, d), lambda i, j: (i, 0))
    return pl.pallas_call(
        functools.partial(_mlp_body, final_norm=final_norm),
        grid=(m // tm, ff // tf),
        in_specs=[row, vec, pl.BlockSpec((d, tf), lambda i, j: (0, j)),
                  pl.BlockSpec((tf, d), lambda i, j: (j, 0)), vec],
        out_specs=row,
        out_shape=jax.ShapeDtypeStruct((m, d), F32),
        scratch_shapes=[pltpu.VMEM((tm, d), BF16), pltpu.VMEM((tm, d), F32)],
        compiler_params=_cparams("parallel", "arbitrary"),
        name="mlp",
    )(x, lp['ln_mlp'], lp['w_up'], lp['w_down'], g_final)


def _pad_cols(a, width):
    return jnp.pad(a, [(0, 0)] * (a.ndim - 1) + [(0, width - a.shape[-1])])


def _pad_rows(a, height):
    return jnp.pad(a, [(0, height - a.shape[0]), (0, 0)])


def _rwkv_cols(a, d):
    e = 3 * d
    pieces = [a[..., :e], _pad_cols(a[..., e:e + LORA_W], LORA_PAD[0]),
              _pad_cols(a[..., e + LORA_W:e + LORA_W + LORA_A], LORA_PAD[1]),
              _pad_cols(a[..., e + LORA_W + LORA_A:], LORA_PAD[2])]
    return jnp.concatenate(pieces, axis=-1)


def _rwkv_cols_inverse(a, d):
    e = 3 * d
    o1 = e + LORA_PAD[0]
    o2 = o1 + LORA_PAD[1]
    return jnp.concatenate([a[..., :e], a[..., e:e + LORA_W], a[..., o1:o1 + LORA_A],
                            a[..., o2:o2 + LORA_G]], axis=-1)


def _layer_params(l, d, ln_mix, w_in, mu_shift, decay_w0, decay_w2, iclr_a0, iclr_a2, gate_g2, k_k, k_a,
                  r_k, lnx_w, lnx_b, lnv_w, lnv_b, w_s, b_s, b_gate, w_branch, w_out, ln_mlp, w_up, w_down):
    shift_w = 3 * d + LORA_W + LORA_A + LORA_G
    e1 = shift_w
    e2 = e1 + 2 * d
    e3 = e2 + 9 * d
    w = w_in[l]
    w_rest = jnp.concatenate([_pad_cols(_rwkv_cols(w[:, :e1], d), 4 * d), w[:, e1:e2], w[:, e3:]], axis=1)
    row = lambda a: a[l].reshape(1, -1)
    return dict(
        ln_mix=row(ln_mix), w_attn=w[:, e2:e3].astype(BF16), w_rest=w_rest.astype(BF16),
        mu_shift=_rwkv_cols(row(mu_shift), d),
        decay_w0=row(decay_w0), decay_w2=_pad_rows(decay_w2[l], LORA_PAD[0]).astype(BF16),
        iclr_a0=row(iclr_a0), iclr_a2=_pad_rows(iclr_a2[l], LORA_PAD[1]).astype(BF16),
        gate_g2=_pad_rows(gate_g2[l], LORA_PAD[2]).astype(BF16),
        k_k=row(k_k), k_a=row(k_a), r_k=row(r_k), lnx_w=row(lnx_w), lnx_b=row(lnx_b),
        lnv_w=row(lnv_w), lnv_b=row(lnv_b), w_s=w_s[l],
        b_s_cols=jnp.repeat(b_s[l].T, d // GMLP_GROUPS, axis=1),
        b_gate=row(b_gate), w_branch=w_branch[l].astype(BF16), w_out=w_out[l].astype(BF16),
        ln_mlp=row(ln_mlp), w_up=w_up[l].astype(BF16), w_down=w_down[l].astype(BF16))


def _tile(n, pref):
    while n % pref:
        pref //= 2
    return pref


def _layer(x, lp, g_final, final_norm, shift_prev, wkv_prev, kv_state, slopes_pair):
    b, t, d = x.shape
    m = b * t
    wr = 3 * d + sum(LORA_PAD)
    off_gmlp, off_gate = 4 * d, 6 * d
    x2 = x.reshape(m, d)
    tm = _tile(m, 1024)
    rest_dtype = BF16 if t % 16 == 0 else F32
    z_all = _rms_matmul(x2, lp['ln_mix'], lp['w_attn'], tm, 512, F32).reshape(b, t, -1)
    z_rest = _rms_matmul(x2, lp['ln_mix'], lp['w_rest'], tm, 512, rest_dtype).reshape(b, t, -1)

    sp = _rwkv_cols(shift_prev, d)[:, None, :]
    parts = _rwkv_prep(z_rest, 0, sp, lp, _tile(t, 128))
    o_a, wkv = _wkv_chunk(parts, lp, wkv_prev, _tile(t, WKV_CHUNK), 16)
    shift_last = _rwkv_cols_inverse(z_rest[:, t - 1, :wr].astype(F32), d)

    if kv_state is None:
        (o_b,), v_rows = _gmlp(z_rest, off_gmlp // (2 * d), lp, GMLP_CHUNK, False), None
        o_c = _attn_prompt(z_all, 0, d, slopes_pair)
        kv_out = []
        for gi, win in enumerate(WINDOWS):
            keep = min(win, t)
            kv = z_all[:, t - keep:, (3 * gi + 1) * d:(3 * gi + 3) * d].reshape(b, keep, 2, d // HEAD, HEAD)
            kv_out.append(jnp.swapaxes(kv, 1, 2))
    else:
        o_b, v_rows = _gmlp(z_rest, off_gmlp // (2 * d), lp, t, True)
        o_c, kv_out = _attn_sample(z_all, 0, d, *kv_state, slopes_pair)

    hmid = _merge(x2, o_a.reshape(m, d), o_b.reshape(m, d), o_c.reshape(m, d),
                  z_rest.reshape(m, -1), off_gate // (3 * d), lp, _tile(m, 256))
    out = _mlp(hmid, lp, g_final, final_norm, _tile(m, 512), 1024)
    return out.reshape(b, t, d), shift_last, wkv, kv_out, v_rows


def kernel(x_prompt, x_sample, state_wkv, state_shift, cache_kv_w128, cache_kv_w512, cache_kv_w2048,
           ln_mix, w_in, mu_shift, decay_w0, decay_w2, iclr_a0, iclr_a2, gate_g2, k_k, k_a, r_k,
           lnx_w, lnx_b, lnv_w, lnv_b, w_s, b_s, b_gate, w_branch, w_out, ln_mlp, w_up, w_down,
           ln_final):
    depth = w_in.shape[0]
    d = x_prompt.shape[-1]
    nh = d // HEAD
    nb_prompt = x_prompt.shape[0]
    shift_w = state_shift.shape[-1]
    slopes = jnp.exp2(-8.0 * jnp.arange(1, nh + 1, dtype=F32) / nh)
    slopes_pair = jnp.repeat(slopes, HEAD).reshape(nh // 2, 1, LANES)
    g_final = ln_final.reshape(1, d)
    caches_t = [jnp.transpose(c, (0, 1, 2, 4, 5, 3)).reshape(c.shape[:3] + (d, c.shape[3]))
                for c in (cache_kv_w128, cache_kv_w512, cache_kv_w2048)]

    xp, xs = x_prompt, x_sample
    outs_p, outs_s = [], []
    adv = None
    for l in range(depth):
        lp = _layer_params(l, d, ln_mix, w_in, mu_shift, decay_w0, decay_w2, iclr_a0, iclr_a2, gate_g2,
                           k_k, k_a, r_k, lnx_w, lnx_b, lnv_w, lnv_b, w_s, b_s, b_gate, w_branch, w_out,
                           ln_mlp, w_up, w_down)
        last = l == depth - 1
        xp, sh, wk, kv, _ = _layer(xp, lp, g_final, last, jnp.zeros((nb_prompt, shift_w), F32),
                                   jnp.zeros((nb_prompt, nh, HEAD, HEAD), F32), None, slopes_pair)
        outs_p.append((wk, sh, kv))
        xs, sh, wk, adv, vr = _layer(xs, lp, g_final, last, state_shift[l], state_wkv[l],
                                     (l, caches_t, adv), slopes_pair)
        outs_s.append((wk, sh, vr))

    stack = lambda f, outs: jnp.stack([f(o) for o in outs])
    s_kv = [jnp.transpose(c.reshape(c.shape[:3] + (nh, HEAD, c.shape[-1])), (0, 1, 2, 5, 3, 4)) for c in adv]
    return (xp, xs,
            stack(lambda o: o[0], outs_p), stack(lambda o: o[1], outs_p),
            stack(lambda o: o[2][0], outs_p), stack(lambda o: o[2][1], outs_p),
            stack(lambda o: o[2][2], outs_p),
            stack(lambda o: o[0], outs_s), stack(lambda o: o[1], outs_s),
            s_kv[0], s_kv[1], s_kv[2], stack(lambda o: o[2], outs_s))
```

```python
import functools

import jax
import jax.numpy as jnp
from jax import lax
from jax.experimental import pallas as pl
from jax.experimental.pallas import tpu as pltpu

F32 = jnp.float32
BF16 = jnp.bfloat16

HEAD = 64
LORA_W, LORA_A, LORA_G = 64, 64, 160
GMLP_CHUNK = 128
GMLP_GROUPS = 8
WINDOWS = (128, 512, 2048)
DILATIONS = (1, 4, 16)
N_BACK = 128
ATTN_BLOCK = 128
ATTN_UNROLL = 4
NORM_EPS = 1e-6
LN_EPS = 1e-5
GN_EPS = 64e-5
MASK_VALUE = -1e30

LANES = 128
VMEM_LIMIT_BYTES = 48 * 2**20

LORA_PAD = (128, 128, 256)
WKV_CHUNK = 64
WKV_SEQS = 2

NN = ((1,), (0,))
NT = ((1,), (1,))
TN = ((0,), (0,))


def _cparams(*sem):
    return pltpu.CompilerParams(dimension_semantics=sem, vmem_limit_bytes=VMEM_LIMIT_BYTES)


def _dot(a, b, dims=NN):
    return lax.dot_general(a.astype(BF16), b.astype(BF16), (dims, ((), ())),
                           preferred_element_type=F32)


def _rms_matmul_body(x_ref, g_ref, w_ref, o_ref, xn_ref):
    @pl.when(pl.program_id(1) == 0)
    def _():
        x = x_ref[...]
        ms = jnp.mean(x * x, axis=-1, keepdims=True)
        xn_ref[...] = (x * lax.rsqrt(ms + NORM_EPS) * g_ref[...]).astype(BF16)

    o_ref[...] = jnp.dot(xn_ref[...], w_ref[...], preferred_element_type=F32).astype(o_ref.dtype)


def _rms_matmul(x, g, w, tm, tn, out_dtype):
    m, d = x.shape
    n = w.shape[1]
    return pl.pallas_call(
        _rms_matmul_body,
        grid=(m // tm, n // tn),
        in_specs=[pl.BlockSpec((tm, d), lambda i, j: (i, 0)),
                  pl.BlockSpec((1, d), lambda i, j: (0, 0)),
                  pl.BlockSpec((d, tn), lambda i, j: (0, j))],
        out_specs=pl.BlockSpec((tm, tn), lambda i, j: (i, j)),
        out_shape=jax.ShapeDtypeStruct((m, n), out_dtype),
        scratch_shapes=[pltpu.VMEM((tm, d), BF16)],
        compiler_params=_cparams("parallel", "arbitrary"),
        name="rms_matmul",
    )(x, g, w)


def _rwkv_prep_body(z_ref, sp_ref, mu_ref, w0_ref, dw2_ref, a0_ref, a2_ref, g2_ref, kk_ref, ka_ref,
                    r_ref, lw_ref, k2_ref, v_ref, kku_ref, a_ref, g_ref, carry_ref, *, d):
    @pl.when(pl.program_id(1) == 0)
    def _():
        carry_ref[...] = sp_ref[...]

    zr = z_ref[...].astype(F32)
    tt = zr.shape[0]
    row = lax.broadcasted_iota(jnp.int32, zr.shape, 0)
    prev = jnp.where(row == 0, carry_ref[...], pltpu.roll(zr, 1, 0))
    carry_ref[...] = zr[tt - 1:tt, :]
    h = zr + (prev - zr) * mu_ref[...]
    e_w = 3 * d
    e_a = e_w + LORA_PAD[0]
    e_g = e_a + LORA_PAD[1]
    k = h[:, d:2 * d]
    xw = w0_ref[...] + _dot(jnp.tanh(h[:, e_w:e_a]), dw2_ref[...])
    softplus = jnp.maximum(-xw, 0.0) + jnp.log1p(jnp.exp(-jnp.abs(xw)))
    a = jax.nn.sigmoid(a0_ref[...] + _dot(h[:, e_a:e_g], a2_ref[...]))
    r_ref[...] = h[:, :d]
    lw_ref[...] = -jnp.exp(-softplus - 0.5)
    k2_ref[...] = k * (1.0 + (a - 1.0) * ka_ref[...])
    v_ref[...] = h[:, 2 * d:3 * d]
    kku_ref[...] = k * kk_ref[...]
    a_ref[...] = a
    g_ref[...] = _dot(jax.nn.sigmoid(h[:, e_g:]), g2_ref[...])


def _rwkv_prep(z_all, col_block, shift_prev, lp, tt):
    b, t, _ = z_all.shape
    d = lp['k_k'].shape[-1]
    wr = 3 * d + sum(LORA_PAD)
    full = lambda a: pl.BlockSpec(a.shape, lambda i, j: (0,) * a.ndim)
    params = [lp['mu_shift'], lp['decay_w0'], lp['decay_w2'], lp['iclr_a0'], lp['iclr_a2'],
              lp['gate_g2'], lp['k_k'], lp['k_a']]
    out_spec = pl.BlockSpec((None, tt, d), lambda i, j: (i, j, 0))
    return pl.pallas_call(
        functools.partial(_rwkv_prep_body, d=d),
        grid=(b, t // tt),
        in_specs=[pl.BlockSpec((None, tt, wr), lambda i, j: (i, j, col_block)),
                  pl.BlockSpec((None, 1, wr), lambda i, j: (i, 0, 0))] + [full(p) for p in params],
        out_specs=[out_spec] * 7,
        out_shape=[jax.ShapeDtypeStruct((b, t, d), F32)] * 7,
        scratch_shapes=[pltpu.VMEM((1, wr), F32)],
        compiler_params=_cparams("parallel", "arbitrary"),
        name="rwkv_prep",
    )(z_all, shift_prev, *params)


def _each(f, *lists):
    return [f(*args) for args in zip(*lists)]


def _unit_lower_solvers(mats):
    c = mats[0].shape[0]
    bs = min(c, 16)
    nb = c // bs
    assert nb in (1, 2, 4)
    ri = lax.broadcasted_iota(jnp.int32, (c, c), 0)
    ci = lax.broadcasted_iota(jnp.int32, (c, c), 1)
    same = (ri // bs) == (ci // bs)
    eye = jnp.where(ri == ci, 1.0, 0.0)
    pw = [jnp.where(same, a, 0.0) for a in mats]
    tinv = [eye - d for d in pw]
    n = 2
    while n < bs:
        pw = _each(_dot, pw, pw)
        tinv = _each(lambda t, x: t + _dot(t, x), tinv, pw)
        n *= 2
    m = _each(lambda t, a: _dot(t, jnp.where(same, 0.0, a)), tinv, mats) if nb > 1 else None

    def solve(rhs):
        x = _each(_dot, tinv, rhs)
        if nb > 2:
            mx = _each(_dot, m, x)
            x = _each(lambda x_, m_, y_: x_ + _dot(m_, y_), x, m, mx)
        if nb > 1:
            x = _each(lambda x_, m_: x_ - _dot(m_, x_), x, m)
        return x

    return solve


def _cumsum_rows(x):
    c = x.shape[0]
    row = lax.broadcasted_iota(jnp.int32, x.shape, 0)
    s = 1
    while s < c:
        x = x + jnp.where(row >= s, pltpu.roll(x, s, 0), 0.0)
        s *= 2
    return x


def _wkv_chunk_body(r_ref, lw_ref, k2_ref, v_ref, kku_ref, a_ref, g_ref, lnw_ref, lnb_ref, rk_ref,
                    s0_ref, o_ref, s_ref):
    @pl.when(pl.program_id(2) == 0)
    def _():
        s_ref[...] = s0_ref[...]

    nseq, c, width = r_ref.shape
    ri = lax.broadcasted_iota(jnp.int32, (c, c), 0)
    ci = lax.broadcasted_iota(jnp.int32, (c, c), 1)
    lower = ri >= ci
    strict = ri > ci
    ek = jnp.where(lax.broadcasted_iota(jnp.int32, (HEAD, HEAD), 0)
                   == lax.broadcasted_iota(jnp.int32, (HEAD, HEAD), 1), 1.0, 0.0)

    lw = [lw_ref[i] for i in range(nseq)]
    cum = [_cumsum_rows(x) for x in lw]
    gam = [jnp.exp(x) for x in cum]
    gam_prev = _each(lambda x, y: jnp.exp(x - y), cum, lw)
    igam = [jnp.exp(-x) for x in cum]
    tail = [jnp.exp(x[c - 1:c, :] - x) for x in cum]

    units = [(i, h) for i in range(nseq) for h in range(width // HEAD)]
    sls = [slice(HEAD * h, HEAD * (h + 1)) for _, h in units]
    cols = lambda x: [x[i][:, sl] for (i, _), sl in zip(units, sls)]
    refcols = lambda x: [x[i, :, sl] for (i, _), sl in zip(units, sls)]
    r, k2, v, kku, av = refcols(r_ref), refcols(k2_ref), refcols(v_ref), refcols(kku_ref), refcols(a_ref)
    kk = [x / jnp.maximum(jnp.sqrt(jnp.sum(x * x, axis=-1, keepdims=True)), 1e-12) for x in kku]
    mul = lambda x, y: x * y
    bvec = _each(mul, kk, av)
    kt = _each(mul, kk, cols(gam_prev))
    rt = _each(mul, r, cols(gam))
    kh = _each(mul, k2, cols(igam))
    bh = _each(mul, bvec, cols(igam))
    kb = _each(mul, k2, cols(tail))
    bb = _each(mul, bvec, cols(tail))

    lhs = _each(lambda x, y: jnp.concatenate([x, y], axis=0), kt, rt)
    sb = _each(lambda x, y: _dot(x, y, NT), lhs, bh)
    sk = _each(lambda x, y: _dot(x, y, NT), lhs, kh)
    a_ub = [jnp.where(strict, x[:c], 0.0) for x in sb]
    a_vk = [jnp.where(strict, x[:c], 0.0) for x in sk]
    b_ub = [jnp.where(lower, x[c:], 0.0) for x in sb]
    b_vk = [jnp.where(lower, x[c:], 0.0) for x in sk]

    solve = _unit_lower_solvers(a_ub)
    av_v = _each(_dot, a_vk, v)
    sol = solve(_each(lambda x, y: jnp.concatenate([x, y], axis=1), kt, av_v))
    bu = _each(_dot, b_ub, sol)
    bv_v = _each(_dot, b_vk, v)
    w_ = [x[:, :HEAD] for x in sol]
    u0 = [x[:, HEAD:] for x in sol]
    q = _each(lambda x, y: x - y[:, :HEAD], rt, bu)
    y0 = _each(lambda x, y: x - y[:, HEAD:], bv_v, bu)
    s_in = [s_ref[i, h] for i, h in units]
    y = _each(lambda q_, s_, y_: _dot(q_, s_, NT) + y_, q, s_in, y0)
    gam_end = cols([x[c - 1:c, :] for x in gam])
    p = _each(lambda w, b_, g_: ek * g_ - _dot(w, b_, TN), w_, bb, gam_end)
    z = _each(lambda v_, u_, kb_, bb_: _dot(jnp.concatenate([v_, -u_], axis=0),
                                            jnp.concatenate([kb_, bb_], axis=0), TN), v, u0, kb, bb)
    s_new = _each(lambda s_, p_, z_: _dot(s_, p_) + z_, s_in, p, z)
    for (i, h), s_ in zip(units, s_new):
        s_ref[i, h] = s_

    mu = [jnp.mean(x, axis=-1, keepdims=True) for x in y]
    yc = _each(lambda x, m_: x - m_, y, mu)
    var = [jnp.mean(x * x, axis=-1, keepdims=True) for x in yc]
    for n, ((i, _), sl) in enumerate(zip(units, sls)):
        yn = yc[n] * lax.rsqrt(var[n] + GN_EPS) * lnw_ref[:, sl] + lnb_ref[:, sl]
        bonus = jnp.sum(r[n] * k2[n] * rk_ref[:, sl], axis=-1, keepdims=True) * v[n]
        o_ref[i, :, sl] = (yn + bonus) * g_ref[i, :, sl]


def _wkv_chunk(parts, lp, s0, chunk, nseq):
    b, t, d = parts[0].shape
    nh = d // HEAD
    tok = pl.BlockSpec((nseq, chunk, d), lambda i, j, c: (i, c, j))
    par = pl.BlockSpec((1, d), lambda i, j, c: (0, j))
    st = pl.BlockSpec((nseq, nh, HEAD, HEAD), lambda i, j, c: (i, j, 0, 0))
    return pl.pallas_call(
        _wkv_chunk_body,
        grid=(b // nseq, 1, t // chunk),
        in_specs=[tok] * 7 + [par] * 3 + [st],
        out_specs=[tok, st],
        out_shape=[jax.ShapeDtypeStruct((b, t, d), F32),
                   jax.ShapeDtypeStruct((b, nh, HEAD, HEAD), F32)],
        compiler_params=_cparams("parallel", "parallel", "arbitrary"),
        name="wkv_chunk",
    )(*parts, lp['lnx_w'], lp['lnx_b'], lp['r_k'], s0)


def _gmlp_body(z_ref, lnw_ref, lnb_ref, ws_ref, bs_ref, o_ref, *vn_out, d):
    z = z_ref[...].astype(F32)
    z = 0.5 * z * (1.0 + lax.erf(z * (2.0 ** -0.5)))
    u, vv = z[:, :d], z[:, d:]
    mu = jnp.mean(vv, axis=-1, keepdims=True)
    vc = vv - mu
    var = jnp.mean(vc * vc, axis=-1, keepdims=True)
    vn = vc * lax.rsqrt(var + LN_EPS) * lnw_ref[...] + lnb_ref[...]
    if vn_out:
        vn_out[0][...] = vn
    c = z.shape[0]
    causal = (lax.broadcasted_iota(jnp.int32, (c, c), 0) >= lax.broadcasted_iota(jnp.int32, (c, c), 1))
    cg = d // GMLP_GROUPS
    for g in range(GMLP_GROUPS):
        sl = slice(cg * g, cg * (g + 1))
        mixed = _dot(jnp.where(causal, ws_ref[g], 0.0), vn[:, sl])
        o_ref[:, sl] = u[:, sl] * (mixed + bs_ref[:, sl])


def _gmlp(z_all, col_block, lp, chunk, want_vn):
    b, t, _ = z_all.shape
    d = lp['lnv_w'].shape[-1]
    ws, bs = lp['w_s'][:, :chunk, :chunk], lp['b_s_cols'][:chunk]
    full = lambda a: pl.BlockSpec(a.shape, lambda i, j: (0,) * a.ndim)
    out_spec = pl.BlockSpec((None, chunk, d), lambda i, j: (i, j, 0))
    n_out = 2 if want_vn else 1
    return pl.pallas_call(
        functools.partial(_gmlp_body, d=d),
        grid=(b, t // chunk),
        in_specs=[pl.BlockSpec((None, chunk, 2 * d), lambda i, j: (i, j, col_block)),
                  full(lp['lnv_w']), full(lp['lnv_b']), full(ws), full(bs)],
        out_specs=[out_spec] * n_out,
        out_shape=[jax.ShapeDtypeStruct((b, t, d), F32)] * n_out,
        compiler_params=_cparams("parallel", "parallel"),
        name="gmlp",
    )(z_all, lp['lnv_w'], lp['lnv_b'], ws, bs)


def _attn_group(slope_ref, q_ref, k_ref, v_ref, o_ref, ls_ref, m_ref, dil):
    t = q_ref.shape[0]
    qb = ATTN_BLOCK
    nbk = t // (dil * qb)

    def own_lanes(hh, nrows):
        lane = lax.broadcasted_iota(jnp.int32, (nrows, LANES), 1)
        return lane < HEAD if hh == 0 else lane >= HEAD

    first = own_lanes(0, qb)
    mine = (first, own_lanes(1, qb))

    def rows(start, size):
        return pl.ds(start, size) if dil == 1 else pl.ds(start, size, stride=dil)

    def make_bias(span, shift):
        rel = (shift + lax.broadcasted_iota(jnp.int32, (qb, span), 0)
               - lax.broadcasted_iota(jnp.int32, (qb, span), 1))
        valid = (rel >= 0) & (rel <= N_BACK)
        dist = (rel * dil).astype(F32)
        return [jnp.where(valid, -slope_ref[:, HEAD * hh:HEAD * hh + 1] * dist, MASK_VALUE)
                for hh in range(2)]

    def update(blocks, bias):
        q = [q_ref[qr, :] * (HEAD ** -0.5) for qr, _ in blocks]
        k = [k_ref[kr, :].astype(BF16) for _, kr in blocks]
        v = [v_ref[kr, :] for _, kr in blocks]
        units = [(u, hh) for u in range(len(blocks)) for hh in range(2)]
        s = [_dot(jnp.where(mine[hh], q[u], 0.0), k[u], NT) + bias[hh] for u, hh in units]
        mp = [m_ref[hh, blocks[u][0], :] for u, hh in units]
        rm = [jnp.broadcast_to(jnp.max(x, axis=-1, keepdims=True), (qb, LANES)) for x in s]
        mn = _each(jnp.maximum, mp, rm)
        alpha = _each(lambda a, b: jnp.exp(a - b), mp, mn)
        p = _each(lambda x, m_: jnp.exp(x - jnp.concatenate([m_] * (x.shape[1] // LANES), axis=1)), s, mn)
        vo = [jnp.where(own_lanes(hh, v[u].shape[0]), v[u], 1.0).astype(BF16) for u, hh in units]
        pv = _each(_dot, p, vo)
        done = []
        for u, (qr, _) in enumerate(blocks):
            a0, a1, r0, r1 = alpha[2 * u], alpha[2 * u + 1], pv[2 * u], pv[2 * u + 1]
            acc = o_ref[qr, :] * jnp.where(first, a0, a1) + jnp.where(first, r0, r1)
            den = ls_ref[qr, :] * jnp.where(first, a1, a0) + jnp.where(first, r1, r0)
            done.append((qr, acc, den, mn[2 * u], mn[2 * u + 1]))
        for qr, acc, den, m0, m1 in done:
            o_ref[qr, :] = acc
            ls_ref[qr, :] = den
            m_ref[0, qr, :] = m0
            m_ref[1, qr, :] = m1

    def run(nblocks, rows_fn, bias):
        main = nblocks // ATTN_UNROLL

        def body(width, base):
            def trip(i, carry):
                update([rows_fn(base + i * width + u) for u in range(width)], bias)
                return carry
            return trip

        if main:
            lax.fori_loop(0, main, body(ATTN_UNROLL, 0), 0)
        if nblocks - main * ATTN_UNROLL:
            body(nblocks - main * ATTN_UNROLL, main * ATTN_UNROLL)(0, 0)

    run(dil, lambda i: (rows(i, qb), rows(i, qb)), make_bias(qb, 0))

    def later(i):
        res = i // (nbk - 1)
        n = 1 + i % (nbk - 1)
        return rows(res + n * (qb * dil), qb), rows(res + (n - 1) * (qb * dil), 2 * qb)

    run(dil * (nbk - 1), later, make_bias(2 * qb, qb))


def _attn_prompt_body(slope_ref, q_ref, k_ref, v_ref, o_ref, ls_ref, m_ref):
    g = pl.program_id(2)

    @pl.when(g == 0)
    def _():
        o_ref[...] = jnp.zeros_like(o_ref)
        ls_ref[...] = jnp.zeros_like(ls_ref)
        m_ref[...] = jnp.full_like(m_ref, MASK_VALUE)

    for gi, dil in enumerate(DILATIONS):
        @pl.when(g == gi)
        def _(dil=dil):
            _attn_group(slope_ref, q_ref, k_ref, v_ref, o_ref, ls_ref, m_ref, dil)

    @pl.when(g == len(DILATIONS) - 1)
    def _():
        o_ref[...] = o_ref[...] / pltpu.roll(ls_ref[...], HEAD, 1)


def _attn_prompt(z_all, col_off, d, slopes):
    b, t, _ = z_all.shape
    assert t % (max(DILATIONS) * 2 * ATTN_BLOCK) == 0
    npair = d // LANES
    base = col_off // LANES
    spec = lambda j: pl.BlockSpec((None, t, LANES),
                                  lambda i, p, g: (i, 0, base + (3 * g + j) * npair + p))
    return pl.pallas_call(
        _attn_prompt_body,
        grid=(b, npair, len(DILATIONS)),
        in_specs=[pl.BlockSpec((None, 1, LANES), lambda i, p, g: (p, 0, 0)), spec(0), spec(1), spec(2)],
        out_specs=pl.BlockSpec((None, t, LANES), lambda i, p, g: (i, 0, p)),
        out_shape=jax.ShapeDtypeStruct((b, t, d), F32),
        scratch_shapes=[pltpu.VMEM((t, LANES), F32), pltpu.VMEM((2, t, LANES), F32)],
        compiler_params=_cparams("parallel", "parallel", "arbitrary"),
        name="attn_prompt",
    )(slopes, z_all, z_all, z_all)


def _attn_sample_body(slope_ref, *refs):
    ng = len(DILATIONS)
    new_refs, newt_refs, cache_refs = refs[:3 * ng], refs[3 * ng:4 * ng], refs[4 * ng:5 * ng]
    o_ref, adv_refs = refs[-ng - 1], refs[-ng:]
    s_len = o_ref.shape[0]
    rows = 2 * s_len
    row = lax.broadcasted_iota(jnp.int32, (rows, LANES), 0)
    lane = lax.broadcasted_iota(jnp.int32, (rows, LANES), 1)
    own = (lane < HEAD) == (row < s_len)
    slope = jnp.where(row[:, :1] < s_len, slope_ref[:, 0:1], slope_ref[:, HEAD:HEAD + 1])
    pad = jnp.zeros((s_len, LANES), F32)
    parts = []
    for gi, dil in enumerate(DILATIONS):
        assert dil & (dil - 1) == 0
        q_ref, kn_ref, vn_ref = new_refs[3 * gi:3 * gi + 3]
        q = q_ref[...] * (HEAD ** -0.5)
        q2 = jnp.where(own, jnp.concatenate([q, q], axis=0), 0.0)
        kt = cache_refs[gi][0]
        vt = cache_refs[gi][1]
        win = kt.shape[1]
        qpos = lax.broadcasted_iota(jnp.int32, (rows, win), 0) % s_len
        gap = qpos + win - lax.broadcasted_iota(jnp.int32, (rows, win), 1)
        ok = (gap <= win) & ((gap & (dil - 1)) == 0)
        sc = jnp.where(ok, _dot(q2, kt) - slope * gap.astype(F32), MASK_VALUE)
        kn = jnp.concatenate([kn_ref[...], pad], axis=0)
        vn = jnp.concatenate([vn_ref[...], pad], axis=0)
        gapn = (lax.broadcasted_iota(jnp.int32, (rows, rows), 0) % s_len
                - lax.broadcasted_iota(jnp.int32, (rows, rows), 1))
        okn = (gapn >= 0) & ((gapn & (dil - 1)) == 0)
        sn = jnp.where(okn, _dot(q2, kn, NT) - slope * gapn.astype(F32), MASK_VALUE)
        m = jnp.maximum(jnp.max(sc, axis=-1, keepdims=True), jnp.max(sn, axis=-1, keepdims=True))
        pc = jnp.exp(sc - m)
        pn = jnp.exp(sn - m)
        den = jnp.sum(pc, axis=-1, keepdims=True) + jnp.sum(pn, axis=-1, keepdims=True)
        num = _dot(pc, vt, NT) + _dot(pn, vn)
        parts.append((m, den, num))
        for j, x in enumerate((kt, vt)):
            adv_refs[gi][j] = pltpu.roll(x, win - s_len, 1)
            adv_refs[gi][j, :, pl.ds(win - s_len, s_len)] = newt_refs[gi][j]
    m_all = functools.reduce(jnp.maximum, [p[0] for p in parts])
    den = sum(p[1] * jnp.exp(p[0] - m_all) for p in parts)
    num = sum(p[2] * jnp.exp(p[0] - m_all) for p in parts)
    o2 = num / den
    o_ref[...] = jnp.where(lane[:s_len] < HEAD, o2[:s_len], o2[s_len:])


def _attn_sample(z_all, col_off, d, layer, caches_t, adv_prev, slopes):
    b, s_len, _ = z_all.shape
    ng = len(DILATIONS)
    npair = d // LANES
    base = col_off // LANES
    new_specs = [pl.BlockSpec((None, s_len, LANES),
                              lambda i, p, g=g, j=j: (i, 0, base + (3 * g + j) * npair + p))
                 for g in range(ng) for j in range(3)]
    newt = [jnp.transpose(z_all[:, :, col_off + (3 * g + 1) * d:col_off + (3 * g + 3) * d]
                          .reshape(b, s_len, 2, d), (0, 2, 3, 1)) for g in range(ng)]
    newt_specs = [pl.BlockSpec((None, 2, LANES, s_len), lambda i, p: (i, 0, p, 0))] * ng
    cache_specs = [pl.BlockSpec((None, None, 2, LANES, c.shape[-1]), lambda i, p: (layer, i, 0, p, 0))
                   for c in caches_t]
    alias_specs = [] if adv_prev is None else [pl.BlockSpec(memory_space=pl.ANY)] * ng
    n_in = 1 + len(new_specs) + 2 * ng
    aliases = {} if adv_prev is None else {n_in + g: 1 + g for g in range(ng)}
    outs = pl.pallas_call(
        _attn_sample_body,
        grid=(b, npair),
        in_specs=[pl.BlockSpec((None, 1, LANES), lambda i, p: (p, 0, 0))] + new_specs + newt_specs
                 + cache_specs + alias_specs,
        out_specs=[pl.BlockSpec((None, s_len, LANES), lambda i, p: (i, 0, p))] + cache_specs,
        out_shape=[jax.ShapeDtypeStruct((b, s_len, d), F32)]
                  + [jax.ShapeDtypeStruct(c.shape, F32) for c in caches_t],
        input_output_aliases=aliases,
        compiler_params=_cparams("parallel", "parallel"),
        name="attn_sample",
    )(slopes, *([z_all] * len(new_specs)), *newt, *caches_t, *(adv_prev or []))
    return outs[0], outs[1:]


def _merge_body(x_ref, oa_ref, ob_ref, oc_ref, zg_ref, bg_ref, wb_ref, wo_ref, o_ref, *, d):
    acc = None
    for n, br in enumerate((oa_ref, ob_ref, oc_ref)):
        sl = slice(n * d, (n + 1) * d)
        term = _dot(br[...], wb_ref[n]) * jax.nn.sigmoid(zg_ref[:, sl].astype(F32) + bg_ref[:, sl])
        acc = term if acc is None else acc + term
    o_ref[...] = x_ref[...] + _dot(acc, wo_ref[...])


def _merge(x, oa, ob, oc, z_all, gate_block, lp, tm):
    m, d = x.shape
    row = pl.BlockSpec((tm, d), lambda i: (i, 0))
    full = lambda a: pl.BlockSpec(a.shape, lambda i: (0,) * a.ndim)
    return pl.pallas_call(
        functools.partial(_merge_body, d=d),
        grid=(m // tm,),
        in_specs=[row, row, row, row, pl.BlockSpec((tm, 3 * d), lambda i: (i, gate_block)),
                  full(lp['b_gate']), full(lp['w_branch']), full(lp['w_out'])],
        out_specs=row,
        out_shape=jax.ShapeDtypeStruct((m, d), F32),
        compiler_params=_cparams("parallel"),
        name="merge",
    )(x, oa, ob, oc, z_all, lp['b_gate'], lp['w_branch'], lp['w_out'])


def _mlp_body(x_ref, g_ref, wu_ref, wd_ref, gf_ref, o_ref, xn_ref, acc_ref, *, final_norm):
    j = pl.program_id(1)

    @pl.when(j == 0)
    def _():
        x = x_ref[...]
        ms = jnp.mean(x * x, axis=-1, keepdims=True)
        xn_ref[...] = (x * lax.rsqrt(ms + NORM_EPS) * g_ref[...]).astype(BF16)
        acc_ref[...] = jnp.zeros_like(acc_ref)

    h = jnp.maximum(jnp.dot(xn_ref[...], wu_ref[...], preferred_element_type=F32), 0.0)
    acc_ref[...] += _dot(h * h, wd_ref[...])

    @pl.when(j == pl.num_programs(1) - 1)
    def _():
        y = x_ref[...] + acc_ref[...]
        if final_norm:
            ms = jnp.mean(y * y, axis=-1, keepdims=True)
            y = y * lax.rsqrt(ms + NORM_EPS) * gf_ref[...]
        o_ref[...] = y


def _mlp(x, lp, g_final, final_norm, tm, tf):
    m, d = x.shape
    ff = lp['w_up'].shape[1]
    vec = pl.BlockSpec((1, d), lambda i, j: (0, 0))
    row = pl.BlockSpec((tm, d), lambda i, j: (i, 0))
    return pl.pallas_call(
        functools.partial(_mlp_body, final_norm=final_norm),
        grid=(m // tm, ff // tf),
        in_specs=[row, vec, pl.BlockSpec((d, tf), lambda i, j: (0, j)),
                  pl.BlockSpec((tf, d), lambda i, j: (j, 0)), vec],
        out_specs=row,
        out_shape=jax.ShapeDtypeStruct((m, d), F32),
        scratch_shapes=[pltpu.VMEM((tm, d), BF16), pltpu.VMEM((tm, d), F32)],
        compiler_params=_cparams("parallel", "arbitrary"),
        name="mlp",
    )(x, lp['ln_mlp'], lp['w_up'], lp['w_down'], g_final)


def _pad_cols(a, width):
    return jnp.pad(a, [(0, 0)] * (a.ndim - 1) + [(0, width - a.shape[-1])])


def _pad_rows(a, height):
    return jnp.pad(a, [(0, height - a.shape[0]), (0, 0)])


def _rwkv_cols(a, d):
    e = 3 * d
    pieces = [a[..., :e], _pad_cols(a[..., e:e + LORA_W], LORA_PAD[0]),
              _pad_cols(a[..., e + LORA_W:e + LORA_W + LORA_A], LORA_PAD[1]),
              _pad_cols(a[..., e + LORA_W + LORA_A:], LORA_PAD[2])]
    return jnp.concatenate(pieces, axis=-1)


def _rwkv_cols_inverse(a, d):
    e = 3 * d
    o1 = e + LORA_PAD[0]
    o2 = o1 + LORA_PAD[1]
    return jnp.concatenate([a[..., :e], a[..., e:e + LORA_W], a[..., o1:o1 + LORA_A],
                            a[..., o2:o2 + LORA_G]], axis=-1)


def _layer_params(l, d, ln_mix, w_in, mu_shift, decay_w0, decay_w2, iclr_a0, iclr_a2, gate_g2, k_k, k_a,
                  r_k, lnx_w, lnx_b, lnv_w, lnv_b, w_s, b_s, b_gate, w_branch, w_out, ln_mlp, w_up, w_down):
    shift_w = 3 * d + LORA_W + LORA_A + LORA_G
    e1 = shift_w
    e2 = e1 + 2 * d
    e3 = e2 + 9 * d
    w = w_in[l]
    w_rest = jnp.concatenate([_pad_cols(_rwkv_cols(w[:, :e1], d), 4 * d), w[:, e1:e2], w[:, e3:]], axis=1)
    row = lambda a: a[l].reshape(1, -1)
    return dict(
        ln_mix=row(ln_mix), w_attn=w[:, e2:e3].astype(BF16), w_rest=w_rest.astype(BF16),
        mu_shift=_rwkv_cols(row(mu_shift), d),
        decay_w0=row(decay_w0), decay_w2=_pad_rows(decay_w2[l], LORA_PAD[0]).astype(BF16),
        iclr_a0=row(iclr_a0), iclr_a2=_pad_rows(iclr_a2[l], LORA_PAD[1]).astype(BF16),
        gate_g2=_pad_rows(gate_g2[l], LORA_PAD[2]).astype(BF16),
        k_k=row(k_k), k_a=row(k_a), r_k=row(r_k), lnx_w=row(lnx_w), lnx_b=row(lnx_b),
        lnv_w=row(lnv_w), lnv_b=row(lnv_b), w_s=w_s[l],
        b_s_cols=jnp.repeat(b_s[l].T, d // GMLP_GROUPS, axis=1),
        b_gate=row(b_gate), w_branch=w_branch[l].astype(BF16), w_out=w_out[l].astype(BF16),
        ln_mlp=row(ln_mlp), w_up=w_up[l].astype(BF16), w_down=w_down[l].astype(BF16))


def _tile(n, pref):
    while n % pref:
        pref //= 2
    return pref


def _layer(x, lp, g_final, final_norm, shift_prev, wkv_prev, kv_state, slopes_pair):
    b, t, d = x.shape
    m = b * t
    wr = 3 * d + sum(LORA_PAD)
    off_gmlp, off_gate = 4 * d, 6 * d
    x2 = x.reshape(m, d)
    tm = _tile(m, 1024)
    rest_dtype = BF16 if t % 16 == 0 else F32
    tn = _tile(9 * d, 1536)
    z_all = _rms_matmul(x2, lp['ln_mix'], lp['w_attn'], tm, tn, F32).reshape(b, t, -1)
    z_rest = _rms_matmul(x2, lp['ln_mix'], lp['w_rest'], tm, tn, rest_dtype).reshape(b, t, -1)

    sp = _rwkv_cols(shift_prev, d)[:, None, :]
    parts = _rwkv_prep(z_rest, 0, sp, lp, _tile(t, 128))
    o_a, wkv = _wkv_chunk(parts, lp, wkv_prev, _tile(t, WKV_CHUNK), _tile(b, WKV_SEQS))
    shift_last = _rwkv_cols_inverse(z_rest[:, t - 1, :wr].astype(F32), d)

    if kv_state is None:
        (o_b,), v_rows = _gmlp(z_rest, off_gmlp // (2 * d), lp, GMLP_CHUNK, False), None
        o_c = _attn_prompt(z_all, 0, d, slopes_pair)
        kv_out = []
        for gi, win in enumerate(WINDOWS):
            keep = min(win, t)
            kv = z_all[:, t - keep:, (3 * gi + 1) * d:(3 * gi + 3) * d].reshape(b, keep, 2, d // HEAD, HEAD)
            kv_out.append(jnp.swapaxes(kv, 1, 2))
    else:
        o_b, v_rows = _gmlp(z_rest, off_gmlp // (2 * d), lp, t, True)
        o_c, kv_out = _attn_sample(z_all, 0, d, *kv_state, slopes_pair)

    hmid = _merge(x2, o_a.reshape(m, d), o_b.reshape(m, d), o_c.reshape(m, d),
                  z_rest.reshape(m, -1), off_gate // (3 * d), lp, _tile(m, 256))
    out = _mlp(hmid, lp, g_final, final_norm, _tile(m, 512), 1024)
    return out.reshape(b, t, d), shift_last, wkv, kv_out, v_rows


def kernel(x_prompt, x_sample, state_wkv, state_shift, cache_kv_w128, cache_kv_w512, cache_kv_w2048,
           ln_mix, w_in, mu_shift, decay_w0, decay_w2, iclr_a0, iclr_a2, gate_g2, k_k, k_a, r_k,
           lnx_w, lnx_b, lnv_w, lnv_b, w_s, b_s, b_gate, w_branch, w_out, ln_mlp, w_up, w_down,
           ln_final):
    depth = w_in.shape[0]
    d = x_prompt.shape[-1]
    nh = d // HEAD
    nb_prompt = x_prompt.shape[0]
    shift_w = state_shift.shape[-1]
    slopes = jnp.exp2(-8.0 * jnp.arange(1, nh + 1, dtype=F32) / nh)
    slopes_pair = jnp.repeat(slopes, HEAD).reshape(nh // 2, 1, LANES)
    g_final = ln_final.reshape(1, d)
    caches_t = [jnp.transpose(c, (0, 1, 2, 4, 5, 3)).reshape(c.shape[:3] + (d, c.shape[3]))
                for c in (cache_kv_w128, cache_kv_w512, cache_kv_w2048)]

    xp, xs = x_prompt, x_sample
    outs_p, outs_s = [], []
    adv = None
    for l in range(depth):
        lp = _layer_params(l, d, ln_mix, w_in, mu_shift, decay_w0, decay_w2, iclr_a0, iclr_a2, gate_g2,
                           k_k, k_a, r_k, lnx_w, lnx_b, lnv_w, lnv_b, w_s, b_s, b_gate, w_branch, w_out,
                           ln_mlp, w_up, w_down)
        last = l == depth - 1
        xp, sh, wk, kv, _ = _layer(xp, lp, g_final, last, jnp.zeros((nb_prompt, shift_w), F32),
                                   jnp.zeros((nb_prompt, nh, HEAD, HEAD), F32), None, slopes_pair)
        outs_p.append((wk, sh, kv))
        xs, sh, wk, adv, vr = _layer(xs, lp, g_final, last, state_shift[l], state_wkv[l],
                                     (l, caches_t, adv), slopes_pair)
        outs_s.append((wk, sh, vr))

    stack = lambda f, outs: jnp.stack([f(o) for o in outs])
    s_kv = [jnp.transpose(c.reshape(c.shape[:3] + (nh, HEAD, c.shape[-1])), (0, 1, 2, 5, 3, 4)) for c in adv]
    return (xp, xs,
            stack(lambda o: o[0], outs_p), stack(lambda o: o[1], outs_p),
            stack(lambda o: o[2][0], outs_p), stack(lambda o: o[2][1], outs_p),
            stack(lambda o: o[2][2], outs_p),
            stack(lambda o: o[0], outs_s), stack(lambda o: o[1], outs_s),
            s_kv[0], s_kv[1], s_kv[2], stack(lambda o: o[2], outs_s))
```

```python
import functools

import jax
import jax.numpy as jnp
from jax import lax
from jax.experimental import pallas as pl
from jax.experimental.pallas import tpu as pltpu

F32 = jnp.float32
BF16 = jnp.bfloat16

HEAD = 64
LORA_W, LORA_A, LORA_G = 64, 64, 160
GMLP_CHUNK = 128
GMLP_GROUPS = 8
WINDOWS = (128, 512, 2048)
DILATIONS = (1, 4, 16)
N_BACK = 128
ATTN_BLOCK = 128
ATTN_UNROLL = 4
NORM_EPS = 1e-6
LN_EPS = 1e-5
GN_EPS = 64e-5
MASK_VALUE = -1e30

LANES = 128
VMEM_LIMIT_BYTES = 48 * 2**20

LORA_PAD = (128, 128, 256)
WKV_CHUNK = 64
WKV_SEQS = 2

NN = ((1,), (0,))
NT = ((1,), (1,))
TN = ((0,), (0,))


def _cparams(*sem):
    return pltpu.CompilerParams(dimension_semantics=sem, vmem_limit_bytes=VMEM_LIMIT_BYTES)


def _dot(a, b, dims=NN):
    return lax.dot_general(a.astype(BF16), b.astype(BF16), (dims, ((), ())),
                           preferred_element_type=F32)


def _rms_matmul_body(x_ref, g_ref, w_ref, o_ref, xn_ref):
    @pl.when(pl.program_id(1) == 0)
    def _():
        x = x_ref[...]
        ms = jnp.mean(x * x, axis=-1, keepdims=True)
        xn_ref[...] = (x * lax.rsqrt(ms + NORM_EPS) * g_ref[...]).astype(BF16)

    o_ref[...] = jnp.dot(xn_ref[...], w_ref[...], preferred_element_type=F32).astype(o_ref.dtype)


def _rms_matmul(x, g, w, tm, tn, out_dtype):
    m, d = x.shape
    n = w.shape[1]
    return pl.pallas_call(
        _rms_matmul_body,
        grid=(m // tm, n // tn),
        in_specs=[pl.BlockSpec((tm, d), lambda i, j: (i, 0)),
                  pl.BlockSpec((1, d), lambda i, j: (0, 0)),
                  pl.BlockSpec((d, tn), lambda i, j: (0, j))],
        out_specs=pl.BlockSpec((tm, tn), lambda i, j: (i, j)),
        out_shape=jax.ShapeDtypeStruct((m, n), out_dtype),
        scratch_shapes=[pltpu.VMEM((tm, d), BF16)],
        compiler_params=_cparams("parallel", "arbitrary"),
        name="rms_matmul",
    )(x, g, w)


def _rwkv_prep_body(z_ref, sp_ref, mu_ref, w0_ref, dw2_ref, a0_ref, a2_ref, g2_ref, kk_ref, ka_ref,
                    r_ref, lw_ref, k2_ref, v_ref, kku_ref, a_ref, g_ref, carry_ref, *, d):
    @pl.when(pl.program_id(1) == 0)
    def _():
        carry_ref[...] = sp_ref[...]

    zr = z_ref[...].astype(F32)
    tt = zr.shape[0]
    row = lax.broadcasted_iota(jnp.int32, zr.shape, 0)
    prev = jnp.where(row == 0, carry_ref[...], pltpu.roll(zr, 1, 0))
    carry_ref[...] = zr[tt - 1:tt, :]
    h = zr + (prev - zr) * mu_ref[...]
    e_w = 3 * d
    e_a = e_w + LORA_PAD[0]
    e_g = e_a + LORA_PAD[1]
    k = h[:, d:2 * d]
    xw = w0_ref[...] + _dot(jnp.tanh(h[:, e_w:e_a]), dw2_ref[...])
    softplus = jnp.maximum(-xw, 0.0) + jnp.log1p(jnp.exp(-jnp.abs(xw)))
    a = jax.nn.sigmoid(a0_ref[...] + _dot(h[:, e_a:e_g], a2_ref[...]))
    r_ref[...] = h[:, :d]
    lw_ref[...] = -jnp.exp(-softplus - 0.5)
    k2_ref[...] = k * (1.0 + (a - 1.0) * ka_ref[...])
    v_ref[...] = h[:, 2 * d:3 * d]
    kku_ref[...] = k * kk_ref[...]
    a_ref[...] = a
    g_ref[...] = _dot(jax.nn.sigmoid(h[:, e_g:]), g2_ref[...])


def _rwkv_prep(z_all, col_block, shift_prev, lp, tt):
    b, t, _ = z_all.shape
    d = lp['k_k'].shape[-1]
    wr = 3 * d + sum(LORA_PAD)
    full = lambda a: pl.BlockSpec(a.shape, lambda i, j: (0,) * a.ndim)
    params = [lp['mu_shift'], lp['decay_w0'], lp['decay_w2'], lp['iclr_a0'], lp['iclr_a2'],
              lp['gate_g2'], lp['k_k'], lp['k_a']]
    out_spec = pl.BlockSpec((None, tt, d), lambda i, j: (i, j, 0))
    return pl.pallas_call(
        functools.partial(_rwkv_prep_body, d=d),
        grid=(b, t // tt),
        in_specs=[pl.BlockSpec((None, tt, wr), lambda i, j: (i, j, col_block)),
                  pl.BlockSpec((None, 1, wr), lambda i, j: (i, 0, 0))] + [full(p) for p in params],
        out_specs=[out_spec] * 7,
        out_shape=[jax.ShapeDtypeStruct((b, t, d), F32)] * 7,
        scratch_shapes=[pltpu.VMEM((1, wr), F32)],
        compiler_params=_cparams("parallel", "arbitrary"),
        name="rwkv_prep",
    )(z_all, shift_prev, *params)


def _each(f, *lists):
    return [f(*args) for args in zip(*lists)]


def _unit_lower_solvers(mats):
    c = mats[0].shape[0]
    bs = min(c, 16)
    nb = c // bs
    assert nb in (1, 2, 4)
    ri = lax.broadcasted_iota(jnp.int32, (c, c), 0)
    ci = lax.broadcasted_iota(jnp.int32, (c, c), 1)
    same = (ri // bs) == (ci // bs)
    eye = jnp.where(ri == ci, 1.0, 0.0)
    pw = [jnp.where(same, a, 0.0) for a in mats]
    tinv = [eye - d for d in pw]
    n = 2
    while n < bs:
        pw = _each(_dot, pw, pw)
        tinv = _each(lambda t, x: t + _dot(t, x), tinv, pw)
        n *= 2
    m = _each(lambda t, a: _dot(t, jnp.where(same, 0.0, a)), tinv, mats) if nb > 1 else None

    def solve(rhs):
        x = _each(_dot, tinv, rhs)
        if nb > 2:
            mx = _each(_dot, m, x)
            x = _each(lambda x_, m_, y_: x_ + _dot(m_, y_), x, m, mx)
        if nb > 1:
            x = _each(lambda x_, m_: x_ - _dot(m_, x_), x, m)
        return x

    return solve


def _cumsum_rows(x):
    c = x.shape[0]
    row = lax.broadcasted_iota(jnp.int32, x.shape, 0)
    s = 1
    while s < c:
        x = x + jnp.where(row >= s, pltpu.roll(x, s, 0), 0.0)
        s *= 2
    return x


def _wkv_chunk_body(r_ref, lw_ref, k2_ref, v_ref, kku_ref, a_ref, g_ref, lnw_ref, lnb_ref, rk_ref,
                    s0_ref, o_ref, s_ref):
    @pl.when(pl.program_id(2) == 0)
    def _():
        s_ref[...] = s0_ref[...]

    nseq, c, width = r_ref.shape
    ri = lax.broadcasted_iota(jnp.int32, (c, c), 0)
    ci = lax.broadcasted_iota(jnp.int32, (c, c), 1)
    lower = ri >= ci
    strict = ri > ci
    ek = jnp.where(lax.broadcasted_iota(jnp.int32, (HEAD, HEAD), 0)
                   == lax.broadcasted_iota(jnp.int32, (HEAD, HEAD), 1), 1.0, 0.0)

    lw = [lw_ref[i] for i in range(nseq)]
    cum = [_cumsum_rows(x) for x in lw]
    gam = [jnp.exp(x) for x in cum]
    gam_prev = _each(lambda x, y: jnp.exp(x - y), cum, lw)
    igam = [jnp.exp(-x) for x in cum]
    tail = [jnp.exp(x[c - 1:c, :] - x) for x in cum]

    units = [(i, h) for i in range(nseq) for h in range(width // HEAD)]
    sls = [slice(HEAD * h, HEAD * (h + 1)) for _, h in units]
    cols = lambda x: [x[i][:, sl] for (i, _), sl in zip(units, sls)]
    refcols = lambda x: [x[i, :, sl] for (i, _), sl in zip(units, sls)]
    r, k2, v, kku, av = refcols(r_ref), refcols(k2_ref), refcols(v_ref), refcols(kku_ref), refcols(a_ref)
    kk = [x / jnp.maximum(jnp.sqrt(jnp.sum(x * x, axis=-1, keepdims=True)), 1e-12) for x in kku]
    mul = lambda x, y: x * y
    bvec = _each(mul, kk, av)
    kt = _each(mul, kk, cols(gam_prev))
    rt = _each(mul, r, cols(gam))
    kh = _each(mul, k2, cols(igam))
    bh = _each(mul, bvec, cols(igam))
    kb = _each(mul, k2, cols(tail))
    bb = _each(mul, bvec, cols(tail))

    lhs = _each(lambda x, y: jnp.concatenate([x, y], axis=0), kt, rt)
    sb = _each(lambda x, y: _dot(x, y, NT), lhs, bh)
    sk = _each(lambda x, y: _dot(x, y, NT), lhs, kh)
    a_ub = [jnp.where(strict, x[:c], 0.0) for x in sb]
    a_vk = [jnp.where(strict, x[:c], 0.0) for x in sk]
    b_ub = [jnp.where(lower, x[c:], 0.0) for x in sb]
    b_vk = [jnp.where(lower, x[c:], 0.0) for x in sk]

    solve = _unit_lower_solvers(a_ub)
    av_v = _each(_dot, a_vk, v)
    sol = solve(_each(lambda x, y: jnp.concatenate([x, y], axis=1), kt, av_v))
    bu = _each(_dot, b_ub, sol)
    bv_v = _each(_dot, b_vk, v)
    w_ = [x[:, :HEAD] for x in sol]
    u0 = [x[:, HEAD:] for x in sol]
    q = _each(lambda x, y: x - y[:, :HEAD], rt, bu)
    y0 = _each(lambda x, y: x - y[:, HEAD:], bv_v, bu)
    s_in = [s_ref[i, h] for i, h in units]
    y = _each(lambda q_, s_, y_: _dot(q_, s_, NT) + y_, q, s_in, y0)
    gam_end = cols([x[c - 1:c, :] for x in gam])
    p = _each(lambda w, b_, g_: ek * g_ - _dot(w, b_, TN), w_, bb, gam_end)
    z = _each(lambda v_, u_, kb_, bb_: _dot(jnp.concatenate([v_, -u_], axis=0),
                                            jnp.concatenate([kb_, bb_], axis=0), TN), v, u0, kb, bb)
    s_new = _each(lambda s_, p_, z_: _dot(s_, p_) + z_, s_in, p, z)
    for (i, h), s_ in zip(units, s_new):
        s_ref[i, h] = s_

    mu = [jnp.mean(x, axis=-1, keepdims=True) for x in y]
    yc = _each(lambda x, m_: x - m_, y, mu)
    var = [jnp.mean(x * x, axis=-1, keepdims=True) for x in yc]
    for n, ((i, _), sl) in enumerate(zip(units, sls)):
        yn = yc[n] * lax.rsqrt(var[n] + GN_EPS) * lnw_ref[:, sl] + lnb_ref[:, sl]
        bonus = jnp.sum(r[n] * k2[n] * rk_ref[:, sl], axis=-1, keepdims=True) * v[n]
        o_ref[i, :, sl] = (yn + bonus) * g_ref[i, :, sl]


def _wkv_chunk(parts, lp, s0, chunk, nseq):
    b, t, d = parts[0].shape
    nh = d // HEAD
    tok = pl.BlockSpec((nseq, chunk, d), lambda i, j, c: (i, c, j))
    par = pl.BlockSpec((1, d), lambda i, j, c: (0, j))
    st = pl.BlockSpec((nseq, nh, HEAD, HEAD), lambda i, j, c: (i, j, 0, 0))
    return pl.pallas_call(
        _wkv_chunk_body,
        grid=(b // nseq, 1, t // chunk),
        in_specs=[tok] * 7 + [par] * 3 + [st],
        out_specs=[tok, st],
        out_shape=[jax.ShapeDtypeStruct((b, t, d), F32),
                   jax.ShapeDtypeStruct((b, nh, HEAD, HEAD), F32)],
        compiler_params=_cparams("parallel", "parallel", "arbitrary"),
        name="wkv_chunk",
    )(*parts, lp['lnx_w'], lp['lnx_b'], lp['r_k'], s0)


def _gmlp_body(z_ref, lnw_ref, lnb_ref, ws_ref, bs_ref, o_ref, *vn_out, d):
    z = z_ref[...].astype(F32)
    z = 0.5 * z * (1.0 + lax.erf(z * (2.0 ** -0.5)))
    u, vv = z[:, :d], z[:, d:]
    mu = jnp.mean(vv, axis=-1, keepdims=True)
    vc = vv - mu
    var = jnp.mean(vc * vc, axis=-1, keepdims=True)
    vn = vc * lax.rsqrt(var + LN_EPS) * lnw_ref[...] + lnb_ref[...]
    if vn_out:
        vn_out[0][...] = vn
    c = z.shape[0]
    causal = (lax.broadcasted_iota(jnp.int32, (c, c), 0) >= lax.broadcasted_iota(jnp.int32, (c, c), 1))
    cg = d // GMLP_GROUPS
    for g in range(GMLP_GROUPS):
        sl = slice(cg * g, cg * (g + 1))
        mixed = _dot(jnp.where(causal, ws_ref[g], 0.0), vn[:, sl])
        o_ref[:, sl] = u[:, sl] * (mixed + bs_ref[:, sl])


def _gmlp(z_all, col_block, lp, chunk, want_vn):
    b, t, _ = z_all.shape
    d = lp['lnv_w'].shape[-1]
    ws, bs = lp['w_s'][:, :chunk, :chunk], lp['b_s_cols'][:chunk]
    full = lambda a: pl.BlockSpec(a.shape, lambda i, j: (0,) * a.ndim)
    out_spec = pl.BlockSpec((None, chunk, d), lambda i, j: (i, j, 0))
    n_out = 2 if want_vn else 1
    return pl.pallas_call(
        functools.partial(_gmlp_body, d=d),
        grid=(b, t // chunk),
        in_specs=[pl.BlockSpec((None, chunk, 2 * d), lambda i, j: (i, j, col_block)),
                  full(lp['lnv_w']), full(lp['lnv_b']), full(ws), full(bs)],
        out_specs=[out_spec] * n_out,
        out_shape=[jax.ShapeDtypeStruct((b, t, d), F32)] * n_out,
        compiler_params=_cparams("parallel", "parallel"),
        name="gmlp",
    )(z_all, lp['lnv_w'], lp['lnv_b'], ws, bs)


def _attn_group(slope_ref, q_ref, k_ref, v_ref, o_ref, ls_ref, m_ref, dil):
    t = q_ref.shape[0]
    qb = ATTN_BLOCK
    nbk = t // (dil * qb)

    def own_lanes(hh, nrows):
        lane = lax.broadcasted_iota(jnp.int32, (nrows, LANES), 1)
        return lane < HEAD if hh == 0 else lane >= HEAD

    first = own_lanes(0, qb)
    mine = (first, own_lanes(1, qb))

    def rows(start, size):
        return pl.ds(start, size) if dil == 1 else pl.ds(start, size, stride=dil)

    def make_bias(span, shift):
        rel = (shift + lax.broadcasted_iota(jnp.int32, (qb, span), 0)
               - lax.broadcasted_iota(jnp.int32, (qb, span), 1))
        valid = (rel >= 0) & (rel <= N_BACK)
        dist = (rel * dil).astype(F32)
        return [jnp.where(valid, -slope_ref[:, HEAD * hh:HEAD * hh + 1] * dist, MASK_VALUE)
                for hh in range(2)]

    def update(blocks, bias):
        q = [q_ref[qr, :] * (HEAD ** -0.5) for qr, _ in blocks]
        k = [k_ref[kr, :].astype(BF16) for _, kr in blocks]
        v = [v_ref[kr, :] for _, kr in blocks]
        units = [(u, hh) for u in range(len(blocks)) for hh in range(2)]
        s = [_dot(jnp.where(mine[hh], q[u], 0.0), k[u], NT) + bias[hh] for u, hh in units]
        mp = [m_ref[hh, blocks[u][0], :] for u, hh in units]
        rm = [jnp.broadcast_to(jnp.max(x, axis=-1, keepdims=True), (qb, LANES)) for x in s]
        mn = _each(jnp.maximum, mp, rm)
        alpha = _each(lambda a, b: jnp.exp(a - b), mp, mn)
        p = _each(lambda x, m_: jnp.exp(x - jnp.concatenate([m_] * (x.shape[1] // LANES), axis=1)), s, mn)
        vo = [jnp.where(own_lanes(hh, v[u].shape[0]), v[u], 1.0).astype(BF16) for u, hh in units]
        pv = _each(_dot, p, vo)
        done = []
        for u, (qr, _) in enumerate(blocks):
            a0, a1, r0, r1 = alpha[2 * u], alpha[2 * u + 1], pv[2 * u], pv[2 * u + 1]
            acc = o_ref[qr, :] * jnp.where(first, a0, a1) + jnp.where(first, r0, r1)
            den = ls_ref[qr, :] * jnp.where(first, a1, a0) + jnp.where(first, r1, r0)
            done.append((qr, acc, den, mn[2 * u], mn[2 * u + 1]))
        for qr, acc, den, m0, m1 in done:
            o_ref[qr, :] = acc
            ls_ref[qr, :] = den
            m_ref[0, qr, :] = m0
            m_ref[1, qr, :] = m1

    def run(nblocks, rows_fn, bias):
        main = nblocks // ATTN_UNROLL

        def body(width, base):
            def trip(i, carry):
                update([rows_fn(base + i * width + u) for u in range(width)], bias)
                return carry
            return trip

        if main:
            lax.fori_loop(0, main, body(ATTN_UNROLL, 0), 0)
        if nblocks - main * ATTN_UNROLL:
            body(nblocks - main * ATTN_UNROLL, main * ATTN_UNROLL)(0, 0)

    run(dil, lambda i: (rows(i, qb), rows(i, qb)), make_bias(qb, 0))

    def later(i):
        res = i // (nbk - 1)
        n = 1 + i % (nbk - 1)
        return rows(res + n * (qb * dil), qb), rows(res + (n - 1) * (qb * dil), 2 * qb)

    run(dil * (nbk - 1), later, make_bias(2 * qb, qb))


def _attn_prompt_body(slope_ref, q_ref, k_ref, v_ref, *refs):
    ng = len(DILATIONS)
    o_ref, win_refs, (ls_ref, m_ref) = refs[-ng - 3], refs[-ng - 2:-2], refs[-2:]
    g = pl.program_id(2)
    t = q_ref.shape[0]

    @pl.when(g == 0)
    def _():
        o_ref[...] = jnp.zeros_like(o_ref)
        ls_ref[...] = jnp.zeros_like(ls_ref)
        m_ref[...] = jnp.full_like(m_ref, MASK_VALUE)

    for gi, dil in enumerate(DILATIONS):
        @pl.when(g == gi)
        def _(gi=gi, dil=dil):
            _attn_group(slope_ref, q_ref, k_ref, v_ref, o_ref, ls_ref, m_ref, dil)
            keep = win_refs[gi].shape[-1]
            win_refs[gi][0] = k_ref[pl.ds(t - keep, keep), :].T
            win_refs[gi][1] = v_ref[pl.ds(t - keep, keep), :].T

    @pl.when(g == ng - 1)
    def _():
        o_ref[...] = o_ref[...] / pltpu.roll(ls_ref[...], HEAD, 1)


def _attn_prompt(z_all, col_off, d, layer, depth, win_prev, slopes):
    b, t, _ = z_all.shape
    assert t % (max(DILATIONS) * 2 * ATTN_BLOCK) == 0
    ng = len(DILATIONS)
    npair = d // LANES
    base = col_off // LANES
    spec = lambda j: pl.BlockSpec((None, t, LANES),
                                  lambda i, p, g: (i, 0, base + (3 * g + j) * npair + p))
    keeps = [min(w, t) for w in WINDOWS]
    win_specs = [pl.BlockSpec((None, None, 2, LANES, k), lambda i, p, g: (layer, i, 0, p, 0)) for k in keeps]
    alias_specs = [] if win_prev is None else [pl.BlockSpec(memory_space=pl.ANY)] * ng
    aliases = {} if win_prev is None else {4 + j: 1 + j for j in range(ng)}
    outs = pl.pallas_call(
        _attn_prompt_body,
        grid=(b, npair, ng),
        in_specs=[pl.BlockSpec((None, 1, LANES), lambda i, p, g: (p, 0, 0)), spec(0), spec(1), spec(2)]
                 + alias_specs,
        out_specs=[pl.BlockSpec((None, t, LANES), lambda i, p, g: (i, 0, p))] + win_specs,
        out_shape=[jax.ShapeDtypeStruct((b, t, d), F32)]
                  + [jax.ShapeDtypeStruct((depth, b, 2, d, k), F32) for k in keeps],
        input_output_aliases=aliases,
        scratch_shapes=[pltpu.VMEM((t, LANES), F32), pltpu.VMEM((2, t, LANES), F32)],
        compiler_params=_cparams("parallel", "parallel", "arbitrary"),
        name="attn_prompt",
    )(slopes, z_all, z_all, z_all, *(win_prev or []))
    return outs[0], outs[1:]


def _attn_sample_body(slope_ref, *refs):
    ng = len(DILATIONS)
    new_refs, newt_refs, cache_refs = refs[:3 * ng], refs[3 * ng:4 * ng], refs[4 * ng:5 * ng]
    o_ref, adv_refs = refs[-ng - 1], refs[-ng:]
    s_len = o_ref.shape[0]
    rows = 2 * s_len
    row = lax.broadcasted_iota(jnp.int32, (rows, LANES), 0)
    lane = lax.broadcasted_iota(jnp.int32, (rows, LANES), 1)
    own = (lane < HEAD) == (row < s_len)
    slope = jnp.where(row[:, :1] < s_len, slope_ref[:, 0:1], slope_ref[:, HEAD:HEAD + 1])
    pad = jnp.zeros((s_len, LANES), F32)
    parts = []
    for gi, dil in enumerate(DILATIONS):
        assert dil & (dil - 1) == 0
        q_ref, kn_ref, vn_ref = new_refs[3 * gi:3 * gi + 3]
        q = q_ref[...] * (HEAD ** -0.5)
        q2 = jnp.where(own, jnp.concatenate([q, q], axis=0), 0.0)
        kt = cache_refs[gi][0]
        vt = cache_refs[gi][1]
        win = kt.shape[1]
        qpos = lax.broadcasted_iota(jnp.int32, (rows, win), 0) % s_len
        gap = qpos + win - lax.broadcasted_iota(jnp.int32, (rows, win), 1)
        ok = (gap <= win) & ((gap & (dil - 1)) == 0)
        sc = jnp.where(ok, _dot(q2, kt) - slope * gap.astype(F32), MASK_VALUE)
        kn = jnp.concatenate([kn_ref[...], pad], axis=0)
        vn = jnp.concatenate([vn_ref[...], pad], axis=0)
        gapn = (lax.broadcasted_iota(jnp.int32, (rows, rows), 0) % s_len
                - lax.broadcasted_iota(jnp.int32, (rows, rows), 1))
        okn = (gapn >= 0) & ((gapn & (dil - 1)) == 0)
        sn = jnp.where(okn, _dot(q2, kn, NT) - slope * gapn.astype(F32), MASK_VALUE)
        m = jnp.maximum(jnp.max(sc, axis=-1, keepdims=True), jnp.max(sn, axis=-1, keepdims=True))
        pc = jnp.exp(sc - m)
        pn = jnp.exp(sn - m)
        den = jnp.sum(pc, axis=-1, keepdims=True) + jnp.sum(pn, axis=-1, keepdims=True)
        num = _dot(pc, vt, NT) + _dot(pn, vn)
        parts.append((m, den, num))
        for j, x in enumerate((kt, vt)):
            adv_refs[gi][j] = pltpu.roll(x, win - s_len, 1)
            adv_refs[gi][j, :, pl.ds(win - s_len, s_len)] = newt_refs[gi][j]
    m_all = functools.reduce(jnp.maximum, [p[0] for p in parts])
    den = sum(p[1] * jnp.exp(p[0] - m_all) for p in parts)
    num = sum(p[2] * jnp.exp(p[0] - m_all) for p in parts)
    o2 = num / den
    o_ref[...] = jnp.where(lane[:s_len] < HEAD, o2[:s_len], o2[s_len:])


def _attn_sample(z_all, col_off, d, layer, caches_t, adv_prev, slopes):
    b, s_len, _ = z_all.shape
    ng = len(DILATIONS)
    npair = d // LANES
    base = col_off // LANES
    new_specs = [pl.BlockSpec((None, s_len, LANES),
                              lambda i, p, g=g, j=j: (i, 0, base + (3 * g + j) * npair + p))
                 for g in range(ng) for j in range(3)]
    newt = [jnp.transpose(z_all[:, :, col_off + (3 * g + 1) * d:col_off + (3 * g + 3) * d]
                          .reshape(b, s_len, 2, d), (0, 2, 3, 1)) for g in range(ng)]
    newt_specs = [pl.BlockSpec((None, 2, LANES, s_len), lambda i, p: (i, 0, p, 0))] * ng
    cache_specs = [pl.BlockSpec((None, None, 2, LANES, c.shape[-1]), lambda i, p: (layer, i, 0, p, 0))
                   for c in caches_t]
    alias_specs = [] if adv_prev is None else [pl.BlockSpec(memory_space=pl.ANY)] * ng
    n_in = 1 + len(new_specs) + 2 * ng
    aliases = {} if adv_prev is None else {n_in + g: 1 + g for g in range(ng)}
    outs = pl.pallas_call(
        _attn_sample_body,
        grid=(b, npair),
        in_specs=[pl.BlockSpec((None, 1, LANES), lambda i, p: (p, 0, 0))] + new_specs + newt_specs
                 + cache_specs + alias_specs,
        out_specs=[pl.BlockSpec((None, s_len, LANES), lambda i, p: (i, 0, p))] + cache_specs,
        out_shape=[jax.ShapeDtypeStruct((b, s_len, d), F32)]
                  + [jax.ShapeDtypeStruct(c.shape, F32) for c in caches_t],
        input_output_aliases=aliases,
        compiler_params=_cparams("parallel", "parallel"),
        name="attn_sample",
    )(slopes, *([z_all] * len(new_specs)), *newt, *caches_t, *(adv_prev or []))
    return outs[0], outs[1:]


def _merge_body(x_ref, oa_ref, ob_ref, oc_ref, zg_ref, bg_ref, wb_ref, wo_ref, o_ref, *, d):
    acc = None
    for n, br in enumerate((oa_ref, ob_ref, oc_ref)):
        sl = slice(n * d, (n + 1) * d)
        term = _dot(br[...], wb_ref[n]) * jax.nn.sigmoid(zg_ref[:, sl].astype(F32) + bg_ref[:, sl])
        acc = term if acc is None else acc + term
    o_ref[...] = x_ref[...] + _dot(acc, wo_ref[...])


def _merge(x, oa, ob, oc, z_all, gate_block, lp, tm):
    m, d = x.shape
    row = pl.BlockSpec((tm, d), lambda i: (i, 0))
    full = lambda a: pl.BlockSpec(a.shape, lambda i: (0,) * a.ndim)
    return pl.pallas_call(
        functools.partial(_merge_body, d=d),
        grid=(m // tm,),
        in_specs=[row, row, row, row, pl.BlockSpec((tm, 3 * d), lambda i: (i, gate_block)),
                  full(lp['b_gate']), full(lp['w_branch']), full(lp['w_out'])],
        out_specs=row,
        out_shape=jax.ShapeDtypeStruct((m, d), F32),
        compiler_params=_cparams("parallel"),
        name="merge",
    )(x, oa, ob, oc, z_all, lp['b_gate'], lp['w_branch'], lp['w_out'])


def _mlp_body(x_ref, g_ref, wu_ref, wd_ref, gf_ref, o_ref, xn_ref, acc_ref, *, final_norm):
    j = pl.program_id(1)

    @pl.when(j == 0)
    def _():
        x = x_ref[...]
        ms = jnp.mean(x * x, axis=-1, keepdims=True)
        xn_ref[...] = (x * lax.rsqrt(ms + NORM_EPS) * g_ref[...]).astype(BF16)
        acc_ref[...] = jnp.zeros_like(acc_ref)

    h = jnp.maximum(jnp.dot(xn_ref[...], wu_ref[...], preferred_element_type=F32), 0.0)
    acc_ref[...] += _dot(h * h, wd_ref[...])

    @pl.when(j == pl.num_programs(1) - 1)
    def _():
        y = x_ref[...] + acc_ref[...]
        if final_norm:
            ms = jnp.mean(y * y, axis=-1, keepdims=True)
            y = y * lax.rsqrt(ms + NORM_EPS) * gf_ref[...]
        o_ref[...] = y


def _mlp(x, lp, g_final, final_norm, tm, tf):
    m, d = x.shape
    ff = lp['w_up'].shape[1]
    vec = pl.BlockSpec((1, d), lambda i, j: (0, 0))
    row = pl.BlockSpec((tm, d), lambda i, j: (i, 0))
    return pl.pallas_call(
        functools.partial(_mlp_body, final_norm=final_norm),
        grid=(m // tm, ff // tf),
        in_specs=[row, vec, pl.BlockSpec((d, tf), lambda i, j: (0, j)),
                  pl.BlockSpec((tf, d), lambda i, j: (j, 0)), vec],
        out_specs=row,
        out_shape=jax.ShapeDtypeStruct((m, d), F32),
        scratch_shapes=[pltpu.VMEM((tm, d), BF16), pltpu.VMEM((tm, d), F32)],
        compiler_params=_cparams("parallel", "arbitrary"),
        name="mlp",
    )(x, lp['ln_mlp'], lp['w_up'], lp['w_down'], g_final)


def _pad_cols(a, width):
    return jnp.pad(a, [(0, 0)] * (a.ndim - 1) + [(0, width - a.shape[-1])])


def _pad_rows(a, height):
    return jnp.pad(a, [(0, height - a.shape[0]), (0, 0)])


def _rwkv_cols(a, d):
    e = 3 * d
    pieces = [a[..., :e], _pad_cols(a[..., e:e + LORA_W], LORA_PAD[0]),
              _pad_cols(a[..., e + LORA_W:e + LORA_W + LORA_A], LORA_PAD[1]),
              _pad_cols(a[..., e + LORA_W + LORA_A:], LORA_PAD[2])]
    return jnp.concatenate(pieces, axis=-1)


def _rwkv_cols_inverse(a, d):
    e = 3 * d
    o1 = e + LORA_PAD[0]
    o2 = o1 + LORA_PAD[1]
    return jnp.concatenate([a[..., :e], a[..., e:e + LORA_W], a[..., o1:o1 + LORA_A],
                            a[..., o2:o2 + LORA_G]], axis=-1)


def _layer_params(l, d, ln_mix, w_in, mu_shift, decay_w0, decay_w2, iclr_a0, iclr_a2, gate_g2, k_k, k_a,
                  r_k, lnx_w, lnx_b, lnv_w, lnv_b, w_s, b_s, b_gate, w_branch, w_out, ln_mlp, w_up, w_down):
    shift_w = 3 * d + LORA_W + LORA_A + LORA_G
    e1 = shift_w
    e2 = e1 + 2 * d
    e3 = e2 + 9 * d
    w = w_in[l]
    w_rest = jnp.concatenate([_pad_cols(_rwkv_cols(w[:, :e1], d), 4 * d), w[:, e1:e2], w[:, e3:]], axis=1)
    row = lambda a: a[l].reshape(1, -1)
    return dict(
        ln_mix=row(ln_mix), w_attn=w[:, e2:e3].astype(BF16), w_rest=w_rest.astype(BF16),
        mu_shift=_rwkv_cols(row(mu_shift), d),
        decay_w0=row(decay_w0), decay_w2=_pad_rows(decay_w2[l], LORA_PAD[0]).astype(BF16),
        iclr_a0=row(iclr_a0), iclr_a2=_pad_rows(iclr_a2[l], LORA_PAD[1]).astype(BF16),
        gate_g2=_pad_rows(gate_g2[l], LORA_PAD[2]).astype(BF16),
        k_k=row(k_k), k_a=row(k_a), r_k=row(r_k), lnx_w=row(lnx_w), lnx_b=row(lnx_b),
        lnv_w=row(lnv_w), lnv_b=row(lnv_b), w_s=w_s[l],
        b_s_cols=jnp.repeat(b_s[l].T, d // GMLP_GROUPS, axis=1),
        b_gate=row(b_gate), w_branch=w_branch[l].astype(BF16), w_out=w_out[l].astype(BF16),
        ln_mlp=row(ln_mlp), w_up=w_up[l].astype(BF16), w_down=w_down[l].astype(BF16))


def _tile(n, pref):
    while n % pref:
        pref //= 2
    return pref


def _layer(x, lp, g_final, final_norm, shift_prev, wkv_prev, layer, depth, caches_t, kv_prev, slopes_pair):
    b, t, d = x.shape
    m = b * t
    wr = 3 * d + sum(LORA_PAD)
    off_gmlp, off_gate = 4 * d, 6 * d
    x2 = x.reshape(m, d)
    tm = _tile(m, 1024)
    rest_dtype = BF16 if t % 16 == 0 else F32
    tn = _tile(9 * d, 2304)
    z_all = _rms_matmul(x2, lp['ln_mix'], lp['w_attn'], tm, tn, F32).reshape(b, t, -1)
    z_rest = _rms_matmul(x2, lp['ln_mix'], lp['w_rest'], tm, tn, rest_dtype).reshape(b, t, -1)

    sp = _rwkv_cols(shift_prev, d)[:, None, :]
    parts = _rwkv_prep(z_rest, 0, sp, lp, _tile(t, 128))
    o_a, wkv = _wkv_chunk(parts, lp, wkv_prev, _tile(t, WKV_CHUNK), _tile(b, WKV_SEQS))
    shift_last = _rwkv_cols_inverse(z_rest[:, t - 1, :wr].astype(F32), d)

    if caches_t is None:
        (o_b,), v_rows = _gmlp(z_rest, off_gmlp // (2 * d), lp, GMLP_CHUNK, False), None
        o_c, kv_out = _attn_prompt(z_all, 0, d, layer, depth, kv_prev, slopes_pair)
    else:
        o_b, v_rows = _gmlp(z_rest, off_gmlp // (2 * d), lp, t, True)
        o_c, kv_out = _attn_sample(z_all, 0, d, layer, caches_t, kv_prev, slopes_pair)

    hmid = _merge(x2, o_a.reshape(m, d), o_b.reshape(m, d), o_c.reshape(m, d),
                  z_rest.reshape(m, -1), off_gate // (3 * d), lp, _tile(m, 256))
    out = _mlp(hmid, lp, g_final, final_norm, _tile(m, 1024), 1024)
    return out.reshape(b, t, d), shift_last, wkv, kv_out, v_rows


def kernel(x_prompt, x_sample, state_wkv, state_shift, cache_kv_w128, cache_kv_w512, cache_kv_w2048,
           ln_mix, w_in, mu_shift, decay_w0, decay_w2, iclr_a0, iclr_a2, gate_g2, k_k, k_a, r_k,
           lnx_w, lnx_b, lnv_w, lnv_b, w_s, b_s, b_gate, w_branch, w_out, ln_mlp, w_up, w_down,
           ln_final):
    depth = w_in.shape[0]
    d = x_prompt.shape[-1]
    nh = d // HEAD
    nb_prompt = x_prompt.shape[0]
    shift_w = state_shift.shape[-1]
    slopes = jnp.exp2(-8.0 * jnp.arange(1, nh + 1, dtype=F32) / nh)
    slopes_pair = jnp.repeat(slopes, HEAD).reshape(nh // 2, 1, LANES)
    g_final = ln_final.reshape(1, d)
    caches_t = [jnp.transpose(c, (0, 1, 2, 4, 5, 3)).reshape(c.shape[:3] + (d, c.shape[3]))
                for c in (cache_kv_w128, cache_kv_w512, cache_kv_w2048)]

    xp, xs = x_prompt, x_sample
    outs_p, outs_s = [], []
    win_p = win_s = None
    for l in range(depth):
        lp = _layer_params(l, d, ln_mix, w_in, mu_shift, decay_w0, decay_w2, iclr_a0, iclr_a2, gate_g2,
                           k_k, k_a, r_k, lnx_w, lnx_b, lnv_w, lnv_b, w_s, b_s, b_gate, w_branch, w_out,
                           ln_mlp, w_up, w_down)
        last = l == depth - 1
        xp, sh, wk, win_p, _ = _layer(xp, lp, g_final, last, jnp.zeros((nb_prompt, shift_w), F32),
                                      jnp.zeros((nb_prompt, nh, HEAD, HEAD), F32), l, depth, None, win_p,
                                      slopes_pair)
        outs_p.append((wk, sh))
        xs, sh, wk, win_s, vr = _layer(xs, lp, g_final, last, state_shift[l], state_wkv[l], l, depth,
                                       caches_t, win_s, slopes_pair)
        outs_s.append((wk, sh, vr))

    stack = lambda f, outs: jnp.stack([f(o) for o in outs])
    rows_major = lambda c: jnp.transpose(c.reshape(c.shape[:3] + (nh, HEAD, c.shape[-1])), (0, 1, 2, 5, 3, 4))
    p_kv, s_kv = [rows_major(c) for c in win_p], [rows_major(c) for c in win_s]
    return (xp, xs,
            stack(lambda o: o[0], outs_p), stack(lambda o: o[1], outs_p),
            p_kv[0], p_kv[1], p_kv[2],
            stack(lambda o: o[0], outs_s), stack(lambda o: o[1], outs_s),
            s_kv[0], s_kv[1], s_kv[2], stack(lambda o: o[2], outs_s))
```
